```python
import math
import jax, jax.numpy as jnp
from jax import lax
import numpy as np

D_MODEL = 1024
BATCH = 2
SEQ = 8192
DEPTH = 4

CHUNK = 64
Q_BLOCK = 128
HEAD_DIM = 64
A_HEADS = 4
A_V = 2 * HEAD_DIM
A_WIDTH = A_HEADS * A_V
B_GROUPS = 4
B_GROUP_DIM = 64
B_WIDTH = B_GROUPS * B_GROUP_DIM
POOL_WINDOWS = (2, 4, 8, 16)
C_HEADS = 4
C_WIDTH = C_HEADS * HEAD_DIM
C_LEFT_CHUNKS = 8
C_BAND = (C_LEFT_CHUNKS + 1) * CHUNK
REL_CLIP = 256
MIX_WIDTH = A_WIDTH + B_WIDTH + C_WIDTH
IN_SPLITS = (A_WIDTH, 2 * A_WIDTH, 3 * A_WIDTH, 3 * A_WIDTH + B_WIDTH,
             3 * A_WIDTH + B_WIDTH + C_WIDTH, 3 * A_WIDTH + B_WIDTH + 2 * C_WIDTH)
IN_WIDTH = 3 * A_WIDTH + B_WIDTH + 3 * C_WIDTH
D_FF = 2816
CONV_WIDTH = 3
EPS = 1e-5
NEG = -1e30

kernel_name = "hybrid_diffattn_pool_chunkattn_convffn"


def rms_norm(x, g):
    xf = x.astype(jnp.float32)
    y = xf * lax.rsqrt(jnp.mean(xf * xf, axis=-1, keepdims=True) + EPS)
    return (y * g.astype(jnp.float32)).astype(x.dtype)


def diff_attention(q, k, v, lam, subln_g, lam_init):
    bsz, s_len, h = q.shape[:3]
    nb = s_len // Q_BLOCK
    q = q * (HEAD_DIM ** -0.5)
    qb = q.reshape(bsz, nb, Q_BLOCK, h, 2, HEAD_DIM).transpose(1, 0, 2, 3, 4, 5)
    k_chunk = jnp.arange(s_len) // CHUNK

    def block(args):
        q_i, i = args
        s = jnp.einsum('bqhmd,bkhmd->bhmqk', q_i, k).astype(jnp.float32)
        q_chunk = (i * Q_BLOCK + jnp.arange(Q_BLOCK)) // CHUNK
        mask = k_chunk[None, :] <= q_chunk[:, None]
        p = jax.nn.softmax(jnp.where(mask, s, NEG), axis=-1)
        a = p[:, :, 0] - lam * p[:, :, 1]
        return jnp.einsum('bhqk,bkhe->bqhe', a.astype(v.dtype), v)

    o = lax.map(block, (qb, jnp.arange(nb)))
    o = o.transpose(1, 0, 2, 3, 4).reshape(bsz, s_len, h, A_V)
    o = rms_norm(o, subln_g) * (1.0 - lam_init)
    return o.reshape(bsz, s_len, A_WIDTH)


def pool_mixer(u, pool_w, pool_scale):
    bsz, s_len, _ = u.shape
    uf = u.astype(jnp.float32).reshape(bsz, s_len, B_GROUPS, B_GROUP_DIM)
    cs = jnp.cumsum(uf, axis=1)
    t = jnp.arange(1, s_len + 1, dtype=jnp.float32)
    outs = []
    for g, w in enumerate(POOL_WINDOWS):
        c = cs[:, :, g]
        lagged = jnp.pad(c, ((0, 0), (w, 0), (0, 0)))[:, :s_len]
        cnt = jnp.minimum(t, float(w))[None, :, None]
        outs.append((c - lagged) / cnt - uf[:, :, g])
    d = jnp.stack(outs, axis=2)
    y = jnp.einsum('bsgc,gcd->bsgd', d, pool_w.astype(jnp.float32))
    y = y.reshape(bsz, s_len, B_WIDTH) * pool_scale.astype(jnp.float32)
    return y.astype(u.dtype)


def chunk_band_attention(q, k, v, rel_bias):
    bsz, s_len, h, d = q.shape
    nc = s_len // CHUNK
    qc = (q * (d ** -0.5)).reshape(bsz, nc, CHUNK, h, d)
    pad = ((0, 0), (C_LEFT_CHUNKS, 0), (0, 0), (0, 0), (0, 0))
    kp = jnp.pad(k.reshape(bsz, nc, CHUNK, h, d), pad)
    vp = jnp.pad(v.reshape(bsz, nc, CHUNK, h, d), pad)
    kb = jnp.concatenate([kp[:, j:j + nc] for j in range(C_LEFT_CHUNKS + 1)], axis=2)
    vb = jnp.concatenate([vp[:, j:j + nc] for j in range(C_LEFT_CHUNKS + 1)], axis=2)
    s = jnp.einsum('bcqhd,bckhd->bhcqk', qc, kb).astype(jnp.float32)
    slot = jnp.arange(C_BAND)
    rel = jnp.arange(CHUNK)[:, None] - (slot - C_LEFT_CHUNKS * CHUNK)[None, :]
    idx = jnp.clip(rel, -REL_CLIP, REL_CLIP) + REL_CLIP
    bias = rel_bias.astype(jnp.float32)[:, idx]
    valid = (jnp.arange(nc)[:, None] + slot[None, :] // CHUNK - C_LEFT_CHUNKS) >= 0
    s = jnp.where(valid[None, None, :, None, :], s + bias[None, :, None], NEG)
    p = jax.nn.softmax(s, axis=-1)
    o = jnp.einsum('bhcqk,bckhd->bcqhd', p.astype(v.dtype), vb)
    return o.reshape(bsz, s_len, C_WIDTH)


def conv_gated_ffn(h, w_up, conv_w, conv_b, w_down):
    s_len = h.shape[1]
    a, g = jnp.split(h @ w_up, 2, axis=-1)
    ap = jnp.pad(a, ((0, 0), (CONV_WIDTH - 1, 0), (0, 0)))
    a = conv_b + conv_w[0] * ap[:, 0:s_len]
    for j in range(1, CONV_WIDTH):
        a = a + conv_w[j] * ap[:, j:j + s_len]
    return (jax.nn.silu(a) * g) @ w_down


def setup_inputs(seed: int = 0) -> dict:
    key = jax.random.key(seed)
    ks = jax.random.split(key, 20)
    n = jax.random.normal
    f32 = jnp.float32
    return {
        "x": n(ks[0], (BATCH, SEQ, D_MODEL), f32),
        "attn_norm": 1.0 + 0.02 * n(ks[1], (DEPTH, D_MODEL), f32),
        "w_in": n(ks[2], (DEPTH, D_MODEL, IN_WIDTH), f32) * D_MODEL ** -0.5,
        "lam_q1": 0.1 * n(ks[3], (DEPTH, HEAD_DIM), f32),
        "lam_k1": 0.1 * n(ks[4], (DEPTH, HEAD_DIM), f32),
        "lam_q2": 0.1 * n(ks[5], (DEPTH, HEAD_DIM), f32),
        "lam_k2": 0.1 * n(ks[6], (DEPTH, HEAD_DIM), f32),
        "diff_subln": 1.0 + 0.02 * n(ks[7], (DEPTH, A_V), f32),
        "pool_w": n(ks[8], (DEPTH, B_GROUPS, B_GROUP_DIM, B_GROUP_DIM), f32) * B_GROUP_DIM ** -0.5,
        "pool_scale": 1.0 + 0.02 * n(ks[9], (DEPTH, B_WIDTH), f32),
        "rel_bias": 0.1 * n(ks[10], (DEPTH, C_HEADS, 2 * REL_CLIP + 1), f32),
        "w_out": n(ks[11], (DEPTH, MIX_WIDTH, D_MODEL), f32) * MIX_WIDTH ** -0.5,
        "ffn_norm": 1.0 + 0.02 * n(ks[12], (DEPTH, D_MODEL), f32),
        "w_up": n(ks[13], (DEPTH, D_MODEL, 2 * D_FF), f32) * D_MODEL ** -0.5,
        "conv_w": n(ks[14], (DEPTH, CONV_WIDTH, D_FF), f32) * CONV_WIDTH ** -0.5,
        "conv_b": 0.01 * n(ks[15], (DEPTH, D_FF), f32),
        "w_down": n(ks[16], (DEPTH, D_FF, D_MODEL), f32) * D_FF ** -0.5,
        "final_norm": 1.0 + 0.02 * n(ks[17], (D_MODEL,), f32),
    }


def reference(x, attn_norm, w_in, lam_q1, lam_k1, lam_q2, lam_k2, diff_subln, pool_w, pool_scale,
              rel_bias, w_out, ffn_norm, w_up, conv_w, conv_b, w_down, final_norm):
    bsz, s_len, _ = x.shape
    for l in range(DEPTH):
        h = rms_norm(x, attn_norm[l])
        proj = h @ w_in[l]
        qa, ka, va, ub, qc, kc, vc = jnp.split(proj, IN_SPLITS, axis=-1)
        lam_init = 0.8 - 0.6 * math.exp(-0.3 * l)
        lam = (jnp.exp(jnp.sum(lam_q1[l].astype(jnp.float32) * lam_k1[l].astype(jnp.float32)))
               - jnp.exp(jnp.sum(lam_q2[l].astype(jnp.float32) * lam_k2[l].astype(jnp.float32)))
               + lam_init)
        out_a = diff_attention(qa.reshape(bsz, s_len, A_HEADS, 2, HEAD_DIM),
                               ka.reshape(bsz, s_len, A_HEADS, 2, HEAD_DIM),
                               va.reshape(bsz, s_len, A_HEADS, A_V),
                               lam, diff_subln[l], lam_init)
        out_b = pool_mixer(ub, pool_w[l], pool_scale[l])
        out_c = chunk_band_attention(qc.reshape(bsz, s_len, C_HEADS, HEAD_DIM),
                                     kc.reshape(bsz, s_len, C_HEADS, HEAD_DIM),
                                     vc.reshape(bsz, s_len, C_HEADS, HEAD_DIM),
                                     rel_bias[l])
        mix = jnp.concatenate([out_a, out_b, out_c], axis=-1)
        x = x + mix @ w_out[l]
        x = x + conv_gated_ffn(rms_norm(x, ffn_norm[l]), w_up[l], conv_w[l], conv_b[l], w_down[l])
    return rms_norm(x, final_norm)
```

```python
import functools
import math

import jax
import jax.numpy as jnp
from jax import lax
from jax.experimental import pallas as pl
from jax.experimental.pallas import tpu as pltpu

F32 = jnp.float32
BF16 = jnp.bfloat16

CHUNK = 64
HEAD_DIM = 64
A_HEADS = 4
A_V = 2 * HEAD_DIM
A_WIDTH = A_HEADS * A_V
B_GROUPS = 4
B_GROUP_DIM = 64
B_WIDTH = B_GROUPS * B_GROUP_DIM
POOL_WINDOWS = (2, 4, 8, 16)
C_HEADS = 4
C_WIDTH = C_HEADS * HEAD_DIM
C_LEFT_CHUNKS = 8
REL_CLIP = 256
EPS = 1e-5
NEG = -1e30

TM_PROJ = 512
TQ_A = 512
TK_A = 512
TQ_C = 256
TM_POST = 512
FF_CHUNK = 256
POOL_HALO = 16
VMEM_LIMIT = 56 * 1024 * 1024


def _rms(x, g):
    ms = jnp.mean(x * x, axis=-1, keepdims=True)
    return x * lax.rsqrt(ms + EPS) * g


def _dot_nt(a, b):
    return lax.dot_general(a, b, (((1,), (1,)), ((), ())), preferred_element_type=F32)


def _in_proj_kernel(x_ref, g_ref, wqk_ref, wvt_ref, wu_ref, wc_ref,
                    qk_ref, vt_ref, u_ref, c_ref, *, tk):
    h = _rms(x_ref[...], g_ref[...]).astype(BF16)
    qk = jnp.dot(h, wqk_ref[...], preferred_element_type=F32)
    col = lax.broadcasted_iota(jnp.int32, (1, qk.shape[1]), 1)
    qk_ref[...] = (qk * jnp.where(col < A_WIDTH, HEAD_DIM ** -0.5, 1.0)).astype(BF16)
    vt = _dot_nt(wvt_ref[...], h).astype(BF16)
    for t in range(vt_ref.shape[0]):
        vt_ref[t] = vt[:, t * tk:(t + 1) * tk]
    u_ref[...] = jnp.dot(h, wu_ref[...], preferred_element_type=F32)
    c = jnp.dot(h, wc_ref[...], preferred_element_type=F32)
    colc = lax.broadcasted_iota(jnp.int32, (1, c.shape[1]), 1)
    c_ref[...] = (c * jnp.where(colc < C_WIDTH, HEAD_DIM ** -0.5, 1.0)).astype(BF16)


def _in_proj(x2, attn_norm, wqk, wvt, wu, wc, layer):
    n, d = x2.shape
    tm, tk = TM_PROJ, TK_A
    lsel = lambda *_: (layer, 0, 0)
    return pl.pallas_call(
        functools.partial(_in_proj_kernel, tk=tk),
        grid=(n // tm,),
        in_specs=[
            pl.BlockSpec((tm, d), lambda i: (i, 0)),
            pl.BlockSpec((None, 1, d), lsel),
            pl.BlockSpec((None,) + wqk.shape[1:], lsel),
            pl.BlockSpec((None,) + wvt.shape[1:], lsel),
            pl.BlockSpec((None,) + wu.shape[1:], lsel),
            pl.BlockSpec((None,) + wc.shape[1:], lsel),
        ],
        out_specs=[
            pl.BlockSpec((tm, 2 * A_WIDTH), lambda i: (i, 0)),
            pl.BlockSpec((tm // tk, A_WIDTH, tk), lambda i: (i, 0, 0)),
            pl.BlockSpec((tm, B_WIDTH), lambda i: (i, 0)),
            pl.BlockSpec((tm, 3 * C_WIDTH), lambda i: (i, 0)),
        ],
        out_shape=[
            jax.ShapeDtypeStruct((n, 2 * A_WIDTH), BF16),
            jax.ShapeDtypeStruct((n // tk, A_WIDTH, tk), BF16),
            jax.ShapeDtypeStruct((n, B_WIDTH), F32),
            jax.ShapeDtypeStruct((n, 3 * C_WIDTH), BF16),
        ],
        compiler_params=pltpu.CompilerParams(
            dimension_semantics=("arbitrary",), vmem_limit_bytes=VMEM_LIMIT),
        name="in_proj",
    )(x2, attn_norm, wqk, wvt, wu, wc)


def _attn_a_kernel(lam_ref, gain_ref, q_ref, k_ref, vt_ref, o_ref, m_ref, l_ref, acc_ref,
                   *, tq, tk, lam_init):
    i = pl.program_id(2)
    q = q_ref[...]
    lane = lax.broadcasted_iota(jnp.int32, q.shape, 1)
    zero = jnp.zeros_like(q)
    qcat = jnp.concatenate([jnp.where(lane < HEAD_DIM, q, zero),
                            jnp.where(lane >= HEAD_DIM, q, zero)], axis=0)
    m_ref[...] = jnp.full(m_ref.shape, NEG, F32)
    l_ref[...] = jnp.zeros(l_ref.shape, F32)
    acc_ref[...] = jnp.zeros(acc_ref.shape, F32)

    def step(j, diagonal):
        start = pl.multiple_of(j * tk, tk)
        s = _dot_nt(k_ref[pl.ds(start, tk), :], qcat)
        if diagonal:
            kc = lax.broadcasted_iota(jnp.int32, s.shape, 0) // CHUNK
            col = lax.broadcasted_iota(jnp.int32, s.shape, 1)
            qc = jnp.where(col >= tq, col - tq, col) // CHUNK
            s = jnp.where(kc <= qc, s, NEG)
        m_old = m_ref[...]
        m_new = jnp.maximum(m_old, jnp.max(s, axis=0, keepdims=True))
        alpha = jnp.exp(m_old - m_new)
        p = jnp.exp(s - m_new)
        l_ref[...] = alpha * l_ref[...] + jnp.sum(p, axis=0, keepdims=True)
        pv = jnp.dot(vt_ref[j], p.astype(BF16), preferred_element_type=F32)
        acc_ref[...] = alpha * acc_ref[...] + pv
        m_ref[...] = m_new

    def body(j, carry):
        step(j, False)
        return carry

    lax.fori_loop(0, i, body, 0)
    step(i, True)

    o = acc_ref[...] / l_ref[...]
    lv = lam_ref[...]
    lam = (jnp.exp(jnp.sum(lv[0:1] * lv[1:2], axis=1, keepdims=True))
           - jnp.exp(jnp.sum(lv[2:3] * lv[3:4], axis=1, keepdims=True)) + lam_init)
    a = o[:, :tq] - lam * o[:, tq:]
    ms = jnp.mean(a * a, axis=0, keepdims=True)
    y = a * lax.rsqrt(ms + EPS) * gain_ref[...] * (1.0 - lam_init)
    o_ref[...] = y.T.astype(o_ref.dtype)


def _attn_a(qk, vt, lamvec, gain, layer, bsz, s_len):
    n = qk.shape[0]
    tq, tk = TQ_A, TK_A
    assert tq == tk
    nq = s_len // tq
    lam_init = 0.8 - 0.6 * math.exp(-0.3 * layer)
    return pl.pallas_call(
        functools.partial(_attn_a_kernel, tq=tq, tk=tk, lam_init=lam_init),
        grid=(bsz, A_HEADS, nq),
        in_specs=[
            pl.BlockSpec((None, 4, HEAD_DIM), lambda b, h, i: (layer, 0, 0)),
            pl.BlockSpec((None, A_V, 1), lambda b, h, i: (layer, 0, 0)),
            pl.BlockSpec((tq, A_V), lambda b, h, i: (b * nq + i, h)),
            pl.BlockSpec((s_len, A_V), lambda b, h, i: (b, A_HEADS + h)),
            pl.BlockSpec((s_len // tk, A_V, tk), lambda b, h, i: (b, h, 0)),
        ],
        out_specs=pl.BlockSpec((tq, A_V), lambda b, h, i: (b * nq + i, h)),
        out_shape=jax.ShapeDtypeStruct((n, A_WIDTH), BF16),
        scratch_shapes=[
            pltpu.VMEM((1, 2 * tq), F32),
            pltpu.VMEM((1, 2 * tq), F32),
            pltpu.VMEM((A_V, 2 * tq), F32),
        ],
        compiler_params=pltpu.CompilerParams(
            dimension_semantics=("arbitrary", "arbitrary", "arbitrary"),
            vmem_limit_bytes=VMEM_LIMIT),
        name="attn_a",
    )(lamvec, gain, qk, qk, vt)


def _mix_bc_kernel(rb_ref, pw_ref, ps_ref, u_ref, up_ref, q_ref,
                   k0_ref, k1_ref, k2_ref, v0_ref, v1_ref, v2_ref,
                   o_ref, bias_ref, *, tq):
    j = pl.program_id(1)
    nk = 3 * tq

    @pl.when((pl.program_id(0) == 0) & (j == 0))
    def _build_bias():
        rb = rb_ref[...]
        lane = lax.broadcasted_iota(jnp.int32, rb.shape, 1)
        edge = rb[:, 2 * REL_CLIP:2 * REL_CLIP + 1]
        ext = jnp.where(lane > 2 * REL_CLIP, edge, rb)
        qc = lax.broadcasted_iota(jnp.int32, (tq, nk), 0) // CHUNK
        kc = lax.broadcasted_iota(jnp.int32, (tq, nk), 1) // CHUNK
        band = (kc >= qc) & (kc <= qc + C_LEFT_CHUNKS)
        for h in range(C_HEADS):
            rows = jnp.broadcast_to(ext[h:h + 1, :], (nk, ext.shape[1]))
            rolled = pltpu.roll(rows, REL_CLIP, axis=1, stride=1, stride_axis=0)
            bias_ref[h] = jnp.where(band, rolled[:, :tq].T, NEG)

    u = u_ref[...]
    halo = jnp.where(j > 0, up_ref[...], 0.0)
    ext_u = jnp.concatenate([halo, u], axis=0)
    s2 = ext_u + pltpu.roll(ext_u, 1, axis=0)
    s4 = s2 + pltpu.roll(s2, 2, axis=0)
    s8 = s4 + pltpu.roll(s4, 4, axis=0)
    s16 = s8 + pltpu.roll(s8, 8, axis=0)
    grp = lax.broadcasted_iota(jnp.int32, (1, B_WIDTH), 1) // B_GROUP_DIM
    wsum = jnp.where(grp == 0, s2, jnp.where(grp == 1, s4, jnp.where(grp == 2, s8, s16)))
    wsum = wsum[POOL_HALO:, :]
    win = jnp.where(grp == 0, 2, jnp.where(grp == 1, 4, jnp.where(grp == 2, 8, 16)))
    pos = j * tq + lax.broadcasted_iota(jnp.int32, (tq, 1), 0)
    cnt = jnp.minimum(pos + 1, win).astype(F32)
    d = wsum / cnt - u
    yb = jnp.dot(d.astype(BF16), pw_ref[...], preferred_element_type=F32) * ps_ref[...]
    o_ref[:, 0:B_WIDTH] = yb.astype(o_ref.dtype)

    q = q_ref[...]
    kk = jnp.concatenate([k0_ref[...], k1_ref[...], k2_ref[...]], axis=0)
    vv = jnp.concatenate([v0_ref[...], v1_ref[...], v2_ref[...]], axis=0)
    kcol = lax.broadcasted_iota(jnp.int32, (1, nk), 1)
    first_valid = jnp.maximum(2 - j, 0) * tq
    kneg = jnp.where(kcol >= first_valid, 0.0, NEG)
    head = lax.broadcasted_iota(jnp.int32, (1, C_WIDTH), 1) // HEAD_DIM
    out = jnp.zeros((tq, C_WIDTH), F32)
    for h in range(C_HEADS):
        qh = jnp.where(head == h, q, jnp.zeros_like(q))
        s = _dot_nt(qh, kk) + bias_ref[h] + kneg
        m = jnp.max(s, axis=1, keepdims=True)
        p = jnp.exp(s - m)
        l = jnp.sum(p, axis=1, keepdims=True)
        oh = jnp.dot(p.astype(BF16), vv, preferred_element_type=F32) / l
        out = jnp.where(head == h, oh, out)
    o_ref[:, B_WIDTH:B_WIDTH + C_WIDTH] = out.astype(o_ref.dtype)


def _mix_bc(u, c, rbp, pw, ps, layer, bsz, s_len):
    n = u.shape[0]
    tq = TQ_C
    assert 2 * tq == C_LEFT_CHUNKS * CHUNK and rbp.shape[-1] == 4 * tq
    nt = s_len // tq
    hpt = tq // POOL_HALO
    cur = lambda col: (lambda b, j: (b * nt + j, col))
    prev = lambda back, col: (lambda b, j: (b * nt + jnp.maximum(j - back, 0), col))
    blk = lambda: (tq, C_WIDTH)
    return pl.pallas_call(
        functools.partial(_mix_bc_kernel, tq=tq),
        grid=(bsz, nt),
        in_specs=[
            pl.BlockSpec((None,) + rbp.shape[1:], lambda b, j: (layer, 0, 0)),
            pl.BlockSpec((None,) + pw.shape[1:], lambda b, j: (layer, 0, 0)),
            pl.BlockSpec((None, 1, B_WIDTH), lambda b, j: (layer, 0, 0)),
            pl.BlockSpec((tq, B_WIDTH), cur(0)),
            pl.BlockSpec((POOL_HALO, B_WIDTH),
                         lambda b, j: ((b * nt + j) * hpt - jnp.minimum(j, 1), 0)),
            pl.BlockSpec(blk(), cur(0)),
            pl.BlockSpec(blk(), prev(2, 1)), pl.BlockSpec(blk(), prev(1, 1)), pl.BlockSpec(blk(), cur(1)),
            pl.BlockSpec(blk(), prev(2, 2)), pl.BlockSpec(blk(), prev(1, 2)), pl.BlockSpec(blk(), cur(2)),
        ],
        out_specs=pl.BlockSpec((tq, B_WIDTH + C_WIDTH), lambda b, j: (b * nt + j, 0)),
        out_shape=jax.ShapeDtypeStruct((n, B_WIDTH + C_WIDTH), BF16),
        scratch_shapes=[pltpu.VMEM((C_HEADS, tq, 3 * tq), F32)],
        compiler_params=pltpu.CompilerParams(
            dimension_semantics=("arbitrary", "arbitrary"), vmem_limit_bytes=VMEM_LIMIT),
        name="mix_bc",
    )(rbp, pw, ps, u, u, c, c, c, c, c, c, c)


def _post_kernel(x_ref, a_ref, bc_ref, wo_ref, g_ref, wa_ref, wg_ref, cw_ref, cb_ref, wd_ref, gf_ref,
                 o_ref, h_ref, acc_ref, carry_ref, *, tiles_per_seq, final_norm):
    i = pl.program_id(0)
    tm = x_ref.shape[0]
    xm = (x_ref[...]
          + jnp.dot(a_ref[...], wo_ref[0:A_WIDTH, :], preferred_element_type=F32)
          + jnp.dot(bc_ref[...], wo_ref[A_WIDTH:, :], preferred_element_type=F32))
    h_ref[...] = _rms(xm, g_ref[...]).astype(BF16)
    acc_ref[...] = xm
    seq_start = (i % tiles_per_seq) == 0
    row = lax.broadcasted_iota(jnp.int32, (tm, 1), 0)

    def chunk(c, carry):
        h = h_ref[...]
        a = jnp.dot(h, wa_ref[c], preferred_element_type=F32)
        g = jnp.dot(h, wg_ref[c], preferred_element_type=F32)
        tail = jnp.where(seq_start, 0.0, carry_ref[c])
        carry_ref[c] = a[tm - 8:, :]
        a1 = jnp.where(row == 0, tail[7:8, :], pltpu.roll(a, 1, axis=0))
        a2 = jnp.where(row == 0, tail[6:7, :],
                       jnp.where(row == 1, tail[7:8, :], pltpu.roll(a, 2, axis=0)))
        cw = cw_ref[c]
        conv = cb_ref[c] + cw[0:1, :] * a2
        conv = conv + cw[1:2, :] * a1
        conv = conv + cw[2:3, :] * a
        hid = (conv * jax.nn.sigmoid(conv) * g).astype(BF16)
        acc_ref[...] += jnp.dot(hid, wd_ref[c], preferred_element_type=F32)
        return carry

    lax.fori_loop(0, wa_ref.shape[0], chunk, 0)
    out = acc_ref[...]
    if final_norm:
        out = _rms(out, gf_ref[...])
    o_ref[...] = out


def _post(x2, oa, obc, wo, ffn_norm, wa, wg, cw, cb, wd, gf, layer, s_len, final_norm):
    n, d = x2.shape
    tm = TM_POST
    nch, _, fc = wa.shape[1:]
    const = pl.Buffered(1)
    l3 = lambda i: (layer, 0, 0)
    l4 = lambda i: (layer, 0, 0, 0)
    return pl.pallas_call(
        functools.partial(_post_kernel, tiles_per_seq=s_len // tm, final_norm=final_norm),
        grid=(n // tm,),
        in_specs=[
            pl.BlockSpec((tm, d), lambda i: (i, 0)),
            pl.BlockSpec((tm, A_WIDTH), lambda i: (i, 0)),
            pl.BlockSpec((tm, B_WIDTH + C_WIDTH), lambda i: (i, 0)),
            pl.BlockSpec((None,) + wo.shape[1:], l3, pipeline_mode=const),
            pl.BlockSpec((None, 1, d), l3),
            pl.BlockSpec((None,) + wa.shape[1:], l4, pipeline_mode=const),
            pl.BlockSpec((None,) + wg.shape[1:], l4, pipeline_mode=const),
            pl.BlockSpec((None,) + cw.shape[1:], l4),
            pl.BlockSpec((None,) + cb.shape[1:], l4),
            pl.BlockSpec((None,) + wd.shape[1:], l4, pipeline_mode=const),
            pl.BlockSpec((1, d), lambda i: (0, 0)),
        ],
        out_specs=pl.BlockSpec((tm, d), lambda i: (i, 0)),
        out_shape=jax.ShapeDtypeStruct((n, d), F32),
        scratch_shapes=[
            pltpu.VMEM((tm, d), BF16),
            pltpu.VMEM((tm, d), F32),
            pltpu.VMEM((nch, 8, fc), F32),
        ],
        compiler_params=pltpu.CompilerParams(
            dimension_semantics=("arbitrary",), vmem_limit_bytes=VMEM_LIMIT),
        name="post",
    )(x2, oa, obc, wo, ffn_norm, wa, wg, cw, cb, wd, gf)


def kernel(x, attn_norm, w_in, lam_q1, lam_k1, lam_q2, lam_k2, diff_subln, pool_w, pool_scale,
           rel_bias, w_out, ffn_norm, w_up, conv_w, conv_b, w_down, final_norm):
    bsz, s_len, d = x.shape
    depth = w_in.shape[0]
    d_ff = w_down.shape[1]
    nch = d_ff // FF_CHUNK
    assert nch * FF_CHUNK == d_ff and s_len % TQ_A == 0 and s_len % TM_POST == 0

    w_in_b = w_in.astype(BF16)
    wqk = w_in_b[:, :, :2 * A_WIDTH]
    wvt = jnp.swapaxes(w_in_b[:, :, 2 * A_WIDTH:3 * A_WIDTH], 1, 2)
    wu = w_in_b[:, :, 3 * A_WIDTH:3 * A_WIDTH + B_WIDTH]
    wc = w_in_b[:, :, 3 * A_WIDTH + B_WIDTH:]
    wo = w_out.astype(BF16)
    w_up_b = w_up.astype(BF16)
    wa = w_up_b[:, :, :d_ff].reshape(depth, d, nch, FF_CHUNK).transpose(0, 2, 1, 3)
    wg = w_up_b[:, :, d_ff:].reshape(depth, d, nch, FF_CHUNK).transpose(0, 2, 1, 3)
    wd = w_down.astype(BF16).reshape(depth, nch, FF_CHUNK, d)
    cw = conv_w.reshape(depth, conv_w.shape[1], nch, FF_CHUNK).transpose(0, 2, 1, 3)
    cb = conv_b.reshape(depth, nch, 1, FF_CHUNK)
    lamvec = jnp.stack([lam_q1, lam_k1, lam_q2, lam_k2], axis=1)
    gain = diff_subln[:, :, None]
    eye = jnp.eye(B_GROUPS, dtype=pool_w.dtype)
    pw = jnp.einsum('lgcd,gh->lgchd', pool_w, eye).reshape(depth, B_WIDTH, B_WIDTH).astype(BF16)
    ps = pool_scale[:, None, :]
    rbp = jnp.pad(rel_bias, ((0, 0), (0, 0), (0, 4 * TQ_C - rel_bias.shape[-1])))
    an = attn_norm[:, None, :]
    fn = ffn_norm[:, None, :]
    gf = final_norm[None, :]

    x2 = x.reshape(bsz * s_len, d)
    for layer in range(depth):
        qk, vt, u, c = _in_proj(x2, an, wqk, wvt, wu, wc, layer)
        oa = _attn_a(qk, vt, lamvec, gain, layer, bsz, s_len)
        obc = _mix_bc(u, c, rbp, pw, ps, layer, bsz, s_len)
        x2 = _post(x2, oa, obc, wo, fn, wa, wg, cw, cb, wd, gf, layer, s_len,
                   final_norm=(layer == depth - 1))
    return x2.reshape(bsz, s_len, d)
```

```python
import functools
import math

import jax
import jax.numpy as jnp
from jax import lax
from jax.experimental import pallas as pl
from jax.experimental.pallas import tpu as pltpu

F32 = jnp.float32
BF16 = jnp.bfloat16

CHUNK = 64
HEAD_DIM = 64
A_HEADS = 4
A_V = 2 * HEAD_DIM
A_WIDTH = A_HEADS * A_V
B_GROUPS = 4
B_GROUP_DIM = 64
B_WIDTH = B_GROUPS * B_GROUP_DIM
POOL_WINDOWS = (2, 4, 8, 16)
C_HEADS = 4
C_WIDTH = C_HEADS * HEAD_DIM
C_LEFT_CHUNKS = 8
REL_CLIP = 256
EPS = 1e-5
NEG = -1e30
LOG2E = math.log2(math.e)
SUM_ROWS = 16

TM_PROJ = 512
TQ_A = 512
TK_A = 512
STRIP_A = 256
TQ_C = 256
TM_POST = 512
FF_CHUNK = 256
POOL_HALO = 16
VMEM_LIMIT = 56 * 1024 * 1024


def _rms(x, g):
    ms = jnp.mean(x * x, axis=-1, keepdims=True)
    return x * lax.rsqrt(ms + EPS) * g


def _dot_nt(a, b):
    return lax.dot_general(a, b, (((1,), (1,)), ((), ())), preferred_element_type=F32)


def _in_proj_kernel(x_ref, g_ref, wqk_ref, wvt_ref, wu_ref, wc_ref,
                    qk_ref, vt_ref, u_ref, c_ref, *, tk):
    h = _rms(x_ref[...], g_ref[...]).astype(BF16)
    qk = jnp.dot(h, wqk_ref[...], preferred_element_type=F32)
    col = lax.broadcasted_iota(jnp.int32, (1, qk.shape[1]), 1)
    qk_ref[...] = (qk * jnp.where(col < A_WIDTH, HEAD_DIM ** -0.5 * LOG2E, 1.0)).astype(BF16)
    vt = _dot_nt(wvt_ref[...], h).astype(BF16)
    rows = A_V + SUM_ROWS
    for t in range(vt_ref.shape[0]):
        for hd in range(A_HEADS):
            vt_ref[t, hd * rows:hd * rows + A_V, :] = vt[hd * A_V:(hd + 1) * A_V, t * tk:(t + 1) * tk]
            vt_ref[t, hd * rows + A_V:(hd + 1) * rows, :] = jnp.ones((SUM_ROWS, tk), BF16)
    u_ref[...] = jnp.dot(h, wu_ref[...], preferred_element_type=F32)
    c = jnp.dot(h, wc_ref[...], preferred_element_type=F32)
    colc = lax.broadcasted_iota(jnp.int32, (1, c.shape[1]), 1)
    c_ref[...] = (c * jnp.where(colc < C_WIDTH, HEAD_DIM ** -0.5, 1.0)).astype(BF16)


def _in_proj(x2, attn_norm, wqk, wvt, wu, wc, layer):
    n, d = x2.shape
    tm, tk = TM_PROJ, TK_A
    lsel = lambda *_: (layer, 0, 0)
    return pl.pallas_call(
        functools.partial(_in_proj_kernel, tk=tk),
        grid=(n // tm,),
        in_specs=[
            pl.BlockSpec((tm, d), lambda i: (i, 0)),
            pl.BlockSpec((None, 1, d), lsel),
            pl.BlockSpec((None,) + wqk.shape[1:], lsel),
            pl.BlockSpec((None,) + wvt.shape[1:], lsel),
            pl.BlockSpec((None,) + wu.shape[1:], lsel),
            pl.BlockSpec((None,) + wc.shape[1:], lsel),
        ],
        out_specs=[
            pl.BlockSpec((tm, 2 * A_WIDTH), lambda i: (i, 0)),
            pl.BlockSpec((tm // tk, A_HEADS * (A_V + SUM_ROWS), tk), lambda i: (i, 0, 0)),
            pl.BlockSpec((tm, B_WIDTH), lambda i: (i, 0)),
            pl.BlockSpec((tm, 3 * C_WIDTH), lambda i: (i, 0)),
        ],
        out_shape=[
            jax.ShapeDtypeStruct((n, 2 * A_WIDTH), BF16),
            jax.ShapeDtypeStruct((n // tk, A_HEADS * (A_V + SUM_ROWS), tk), BF16),
            jax.ShapeDtypeStruct((n, B_WIDTH), F32),
            jax.ShapeDtypeStruct((n, 3 * C_WIDTH), BF16),
        ],
        compiler_params=pltpu.CompilerParams(
            dimension_semantics=("arbitrary",), vmem_limit_bytes=VMEM_LIMIT),
        name="in_proj",
    )(x2, attn_norm, wqk, wvt, wu, wc)


def _attn_a_kernel(lam_ref, gain_ref, q_ref, k_ref, vt_ref, o_ref,
                   qcat_ref, sa_ref, sb_ref, m_ref, l_ref, acc_ref,
                   *, tq, tk, strip, lam_init):
    i = pl.program_id(2)
    q = q_ref[...]
    lane = lax.broadcasted_iota(jnp.int32, q.shape, 1)
    zero = jnp.zeros_like(q)
    qcat_ref[0:tq, :] = jnp.where(lane < HEAD_DIM, q, zero)
    qcat_ref[tq:2 * tq, :] = jnp.where(lane >= HEAD_DIM, q, zero)
    m_ref[...] = jnp.full(m_ref.shape, NEG, F32)
    l_ref[...] = jnp.zeros(l_ref.shape, F32)
    acc_ref[...] = jnp.zeros(acc_ref.shape, F32)
    nstrip = 2 * tq // strip

    def scores(j, c, dst_ref):
        cs = slice(c * strip, (c + 1) * strip)
        k_t = k_ref[pl.ds(pl.multiple_of(j * tk, tk), tk), :]
        dst_ref[:, cs] = _dot_nt(k_t, qcat_ref[cs, :])

    def consume(t, c, src_ref, diagonal):
        cs = slice(c * strip, (c + 1) * strip)
        s = src_ref[:, cs]
        if diagonal:
            kc = lax.broadcasted_iota(jnp.int32, s.shape, 0) // CHUNK
            qc = ((c * strip) % tq + lax.broadcasted_iota(jnp.int32, s.shape, 1)) // CHUNK
            s = jnp.where(kc <= qc, s, NEG)
        m_old = m_ref[:, cs]
        m_new = jnp.maximum(m_old, jnp.max(s, axis=0, keepdims=True))
        alpha = jnp.exp2(m_old - m_new)
        p = jnp.exp2(s - m_new).astype(BF16)
        pv = jnp.dot(vt_ref[t], p, preferred_element_type=F32)
        l_ref[:, cs] = alpha * l_ref[:, cs] + pv[A_V:A_V + 1, :]
        acc_ref[:, cs] = alpha * acc_ref[:, cs] + pv[:A_V, :]
        m_ref[:, cs] = m_new

    def pipelined(j, dst_ref, src_ref):
        for c in range(nstrip):
            scores(j, c, dst_ref)
            consume(j - 1, c, src_ref, False)

    for c in range(nstrip):
        scores(0, c, sa_ref)

    def body(j, carry):
        @pl.when(j % 2 == 1)
        def _():
            pipelined(j, sb_ref, sa_ref)

        @pl.when(j % 2 == 0)
        def _():
            pipelined(j, sa_ref, sb_ref)
        return carry

    lax.fori_loop(1, i + 1, body, 0)

    @pl.when(i % 2 == 0)
    def _():
        for c in range(nstrip):
            consume(i, c, sa_ref, True)

    @pl.when(i % 2 == 1)
    def _():
        for c in range(nstrip):
            consume(i, c, sb_ref, True)

    o = acc_ref[...] / l_ref[...]
    lv = lam_ref[...]
    lam = (jnp.exp(jnp.sum(lv[0:1] * lv[1:2], axis=1, keepdims=True))
           - jnp.exp(jnp.sum(lv[2:3] * lv[3:4], axis=1, keepdims=True)) + lam_init)
    a = o[:, :tq] - lam * o[:, tq:]
    ms = jnp.mean(a * a, axis=0, keepdims=True)
    y = a * lax.rsqrt(ms + EPS) * gain_ref[...] * (1.0 - lam_init)
    o_ref[...] = y.T.astype(o_ref.dtype)


def _attn_a(qk, vt, lamvec, gain, layer, bsz, s_len):
    n = qk.shape[0]
    tq, tk = TQ_A, TK_A
    assert tq == tk
    nq = s_len // tq
    lam_init = 0.8 - 0.6 * math.exp(-0.3 * layer)
    return pl.pallas_call(
        functools.partial(_attn_a_kernel, tq=tq, tk=tk, strip=STRIP_A, lam_init=lam_init),
        grid=(bsz, A_HEADS, nq),
        in_specs=[
            pl.BlockSpec((None, 4, HEAD_DIM), lambda b, h, i: (layer, 0, 0)),
            pl.BlockSpec((None, A_V, 1), lambda b, h, i: (layer, 0, 0)),
            pl.BlockSpec((tq, A_V), lambda b, h, i: (b * nq + i, h)),
            pl.BlockSpec((s_len, A_V), lambda b, h, i: (b, A_HEADS + h)),
            pl.BlockSpec((s_len // tk, A_V + SUM_ROWS, tk), lambda b, h, i: (b, h, 0)),
        ],
        out_specs=pl.BlockSpec((tq, A_V), lambda b, h, i: (b * nq + i, h)),
        out_shape=jax.ShapeDtypeStruct((n, A_WIDTH), BF16),
        scratch_shapes=[
            pltpu.VMEM((2 * tq, A_V), BF16),
            pltpu.VMEM((tk, 2 * tq), F32),
            pltpu.VMEM((tk, 2 * tq), F32),
            pltpu.VMEM((1, 2 * tq), F32),
            pltpu.VMEM((1, 2 * tq), F32),
            pltpu.VMEM((A_V, 2 * tq), F32),
        ],
        compiler_params=pltpu.CompilerParams(
            dimension_semantics=("arbitrary", "arbitrary", "arbitrary"),
            vmem_limit_bytes=VMEM_LIMIT),
        name="attn_a",
    )(lamvec, gain, qk, qk, vt)


def _mix_bc_kernel(rb_ref, pw_ref, ps_ref, u_ref, up_ref, q_ref,
                   k0_ref, k1_ref, k2_ref, v0_ref, v1_ref, v2_ref,
                   o_ref, bias_ref, *, tq):
    j = pl.program_id(1)
    nk = 3 * tq

    @pl.when((pl.program_id(0) == 0) & (j == 0))
    def _build_bias():
        rb = rb_ref[...]
        lane = lax.broadcasted_iota(jnp.int32, rb.shape, 1)
        edge = rb[:, 2 * REL_CLIP:2 * REL_CLIP + 1]
        ext = jnp.where(lane > 2 * REL_CLIP, edge, rb)
        qc = lax.broadcasted_iota(jnp.int32, (tq, nk), 0) // CHUNK
        kc = lax.broadcasted_iota(jnp.int32, (tq, nk), 1) // CHUNK
        band = (kc >= qc) & (kc <= qc + C_LEFT_CHUNKS)
        for h in range(C_HEADS):
            rows = jnp.broadcast_to(ext[h:h + 1, :], (nk, ext.shape[1]))
            rolled = pltpu.roll(rows, REL_CLIP, axis=1, stride=1, stride_axis=0)
            bias_ref[h] = jnp.where(band, rolled[:, :tq].T, NEG)

    u = u_ref[...]
    halo = jnp.where(j > 0, up_ref[...], 0.0)
    ext_u = jnp.concatenate([halo, u], axis=0)
    s2 = ext_u + pltpu.roll(ext_u, 1, axis=0)
    s4 = s2 + pltpu.roll(s2, 2, axis=0)
    s8 = s4 + pltpu.roll(s4, 4, axis=0)
    s16 = s8 + pltpu.roll(s8, 8, axis=0)
    grp = lax.broadcasted_iota(jnp.int32, (1, B_WIDTH), 1) // B_GROUP_DIM
    wsum = jnp.where(grp == 0, s2, jnp.where(grp == 1, s4, jnp.where(grp == 2, s8, s16)))
    wsum = wsum[POOL_HALO:, :]
    win = jnp.where(grp == 0, 2, jnp.where(grp == 1, 4, jnp.where(grp == 2, 8, 16)))
    pos = j * tq + lax.broadcasted_iota(jnp.int32, (tq, 1), 0)
    cnt = jnp.minimum(pos + 1, win).astype(F32)
    d = wsum / cnt - u
    yb = jnp.dot(d.astype(BF16), pw_ref[...], preferred_element_type=F32) * ps_ref[...]
    o_ref[:, 0:B_WIDTH] = yb.astype(o_ref.dtype)

    q = q_ref[...]
    kk = jnp.concatenate([k0_ref[...], k1_ref[...], k2_ref[...]], axis=0)
    vv = jnp.concatenate([v0_ref[...], v1_ref[...], v2_ref[...]], axis=0)
    kcol = lax.broadcasted_iota(jnp.int32, (1, nk), 1)
    first_valid = jnp.maximum(2 - j, 0) * tq
    kneg = jnp.where(kcol >= first_valid, 0.0, NEG)
    head = lax.broadcasted_iota(jnp.int32, (1, C_WIDTH), 1) // HEAD_DIM
    out = jnp.zeros((tq, C_WIDTH), F32)
    for h in range(C_HEADS):
        qh = jnp.where(head == h, q, jnp.zeros_like(q))
        s = _dot_nt(qh, kk) + bias_ref[h] + kneg
        m = jnp.max(s, axis=1, keepdims=True)
        p = jnp.exp(s - m)
        l = jnp.sum(p, axis=1, keepdims=True)
        oh = jnp.dot(p.astype(BF16), vv, preferred_element_type=F32) / l
        out = jnp.where(head == h, oh, out)
    o_ref[:, B_WIDTH:B_WIDTH + C_WIDTH] = out.astype(o_ref.dtype)


def _mix_bc(u, c, rbp, pw, ps, layer, bsz, s_len):
    n = u.shape[0]
    tq = TQ_C
    assert 2 * tq == C_LEFT_CHUNKS * CHUNK and rbp.shape[-1] == 4 * tq
    nt = s_len // tq
    hpt = tq // POOL_HALO
    cur = lambda col: (lambda b, j: (b * nt + j, col))
    prev = lambda back, col: (lambda b, j: (b * nt + jnp.maximum(j - back, 0), col))
    blk = lambda: (tq, C_WIDTH)
    return pl.pallas_call(
        functools.partial(_mix_bc_kernel, tq=tq),
        grid=(bsz, nt),
        in_specs=[
            pl.BlockSpec((None,) + rbp.shape[1:], lambda b, j: (layer, 0, 0)),
            pl.BlockSpec((None,) + pw.shape[1:], lambda b, j: (layer, 0, 0)),
            pl.BlockSpec((None, 1, B_WIDTH), lambda b, j: (layer, 0, 0)),
            pl.BlockSpec((tq, B_WIDTH), cur(0)),
            pl.BlockSpec((POOL_HALO, B_WIDTH),
                         lambda b, j: ((b * nt + j) * hpt - jnp.minimum(j, 1), 0)),
            pl.BlockSpec(blk(), cur(0)),
            pl.BlockSpec(blk(), prev(2, 1)), pl.BlockSpec(blk(), prev(1, 1)), pl.BlockSpec(blk(), cur(1)),
            pl.BlockSpec(blk(), prev(2, 2)), pl.BlockSpec(blk(), prev(1, 2)), pl.BlockSpec(blk(), cur(2)),
        ],
        out_specs=pl.BlockSpec((tq, B_WIDTH + C_WIDTH), lambda b, j: (b * nt + j, 0)),
        out_shape=jax.ShapeDtypeStruct((n, B_WIDTH + C_WIDTH), BF16),
        scratch_shapes=[pltpu.VMEM((C_HEADS, tq, 3 * tq), F32)],
        compiler_params=pltpu.CompilerParams(
            dimension_semantics=("arbitrary", "arbitrary"), vmem_limit_bytes=VMEM_LIMIT),
        name="mix_bc",
    )(rbp, pw, ps, u, u, c, c, c, c, c, c, c)


def _post_kernel(x_ref, a_ref, bc_ref, wo_ref, g_ref, wa_ref, wg_ref, cw_ref, cb_ref, wd_ref, gf_ref,
                 o_ref, h_ref, acc_ref, carry_ref, a0_ref, g0_ref, a1_ref, g1_ref,
                 *, tiles_per_seq, final_norm):
    i = pl.program_id(0)
    tm = x_ref.shape[0]
    xm = (x_ref[...]
          + jnp.dot(a_ref[...], wo_ref[0:A_WIDTH, :], preferred_element_type=F32)
          + jnp.dot(bc_ref[...], wo_ref[A_WIDTH:, :], preferred_element_type=F32))
    h_ref[...] = _rms(xm, g_ref[...]).astype(BF16)
    acc_ref[...] = xm
    seq_start = (i % tiles_per_seq) == 0
    row = lax.broadcasted_iota(jnp.int32, (tm, 1), 0)

    def up(c, a_ref, g_ref):
        h = h_ref[...]
        a_ref[...] = jnp.dot(h, wa_ref[c], preferred_element_type=F32)
        g_ref[...] = jnp.dot(h, wg_ref[c], preferred_element_type=F32)

    def down(c, a_ref, g_ref):
        a = a_ref[...]
        tail = jnp.where(seq_start, 0.0, carry_ref[c])
        carry_ref[c] = a[tm - 8:, :]
        a1 = jnp.where(row == 0, tail[7:8, :], pltpu.roll(a, 1, axis=0))
        a2 = jnp.where(row == 0, tail[6:7, :],
                       jnp.where(row == 1, tail[7:8, :], pltpu.roll(a, 2, axis=0)))
        cw = cw_ref[c]
        conv = cb_ref[c] + cw[0:1, :] * a2
        conv = conv + cw[1:2, :] * a1
        conv = conv + cw[2:3, :] * a
        hid = (conv * jax.nn.sigmoid(conv) * g_ref[...]).astype(BF16)
        acc_ref[...] += jnp.dot(hid, wd_ref[c], preferred_element_type=F32)

    nch = wa_ref.shape[0]
    assert nch % 2 == 1
    up(0, a0_ref, g0_ref)

    def pair(pp, carry):
        c = 2 * pp
        up(c + 1, a1_ref, g1_ref)
        down(c, a0_ref, g0_ref)
        up(c + 2, a0_ref, g0_ref)
        down(c + 1, a1_ref, g1_ref)
        return carry

    lax.fori_loop(0, nch // 2, pair, 0)
    down(nch - 1, a0_ref, g0_ref)
    out = acc_ref[...]
    if final_norm:
        out = _rms(out, gf_ref[...])
    o_ref[...] = out


def _post(x2, oa, obc, wo, ffn_norm, wa, wg, cw, cb, wd, gf, layer, s_len, final_norm):
    n, d = x2.shape
    tm = TM_POST
    nch, _, fc = wa.shape[1:]
    const = pl.Buffered(1)
    l3 = lambda i: (layer, 0, 0)
    l4 = lambda i: (layer, 0, 0, 0)
    return pl.pallas_call(
        functools.partial(_post_kernel, tiles_per_seq=s_len // tm, final_norm=final_norm),
        grid=(n // tm,),
        in_specs=[
            pl.BlockSpec((tm, d), lambda i: (i, 0)),
            pl.BlockSpec((tm, A_WIDTH), lambda i: (i, 0)),
            pl.BlockSpec((tm, B_WIDTH + C_WIDTH), lambda i: (i, 0)),
            pl.BlockSpec((None,) + wo.shape[1:], l3, pipeline_mode=const),
            pl.BlockSpec((None, 1, d), l3),
            pl.BlockSpec((None,) + wa.shape[1:], l4, pipeline_mode=const),
            pl.BlockSpec((None,) + wg.shape[1:], l4, pipeline_mode=const),
            pl.BlockSpec((None,) + cw.shape[1:], l4),
            pl.BlockSpec((None,) + cb.shape[1:], l4),
            pl.BlockSpec((None,) + wd.shape[1:], l4, pipeline_mode=const),
            pl.BlockSpec((1, d), lambda i: (0, 0)),
        ],
        out_specs=pl.BlockSpec((tm, d), lambda i: (i, 0)),
        out_shape=jax.ShapeDtypeStruct((n, d), F32),
        scratch_shapes=[
            pltpu.VMEM((tm, d), BF16),
            pltpu.VMEM((tm, d), F32),
            pltpu.VMEM((nch, 8, fc), F32),
            pltpu.VMEM((tm, fc), F32), pltpu.VMEM((tm, fc), F32),
            pltpu.VMEM((tm, fc), F32), pltpu.VMEM((tm, fc), F32),
        ],
        compiler_params=pltpu.CompilerParams(
            dimension_semantics=("arbitrary",), vmem_limit_bytes=VMEM_LIMIT),
        name="post",
    )(x2, oa, obc, wo, ffn_norm, wa, wg, cw, cb, wd, gf)


def kernel(x, attn_norm, w_in, lam_q1, lam_k1, lam_q2, lam_k2, diff_subln, pool_w, pool_scale,
           rel_bias, w_out, ffn_norm, w_up, conv_w, conv_b, w_down, final_norm):
    bsz, s_len, d = x.shape
    depth = w_in.shape[0]
    d_ff = w_down.shape[1]
    nch = d_ff // FF_CHUNK
    assert nch * FF_CHUNK == d_ff and s_len % TQ_A == 0 and s_len % TM_POST == 0

    w_in_b = w_in.astype(BF16)
    wqk = w_in_b[:, :, :2 * A_WIDTH]
    wvt = jnp.swapaxes(w_in_b[:, :, 2 * A_WIDTH:3 * A_WIDTH], 1, 2)
    wu = w_in_b[:, :, 3 * A_WIDTH:3 * A_WIDTH + B_WIDTH]
    wc = w_in_b[:, :, 3 * A_WIDTH + B_WIDTH:]
    wo = w_out.astype(BF16)
    w_up_b = w_up.astype(BF16)
    wa = w_up_b[:, :, :d_ff].reshape(depth, d, nch, FF_CHUNK).transpose(0, 2, 1, 3)
    wg = w_up_b[:, :, d_ff:].reshape(depth, d, nch, FF_CHUNK).transpose(0, 2, 1, 3)
    wd = w_down.astype(BF16).reshape(depth, nch, FF_CHUNK, d)
    cw = conv_w.reshape(depth, conv_w.shape[1], nch, FF_CHUNK).transpose(0, 2, 1, 3)
    cb = conv_b.reshape(depth, nch, 1, FF_CHUNK)
    lamvec = jnp.stack([lam_q1, lam_k1, lam_q2, lam_k2], axis=1)
    gain = diff_subln[:, :, None]
    eye = jnp.eye(B_GROUPS, dtype=pool_w.dtype)
    pw = jnp.einsum('lgcd,gh->lgchd', pool_w, eye).reshape(depth, B_WIDTH, B_WIDTH).astype(BF16)
    ps = pool_scale[:, None, :]
    rbp = jnp.pad(rel_bias, ((0, 0), (0, 0), (0, 4 * TQ_C - rel_bias.shape[-1])))
    an = attn_norm[:, None, :]
    fn = ffn_norm[:, None, :]
    gf = final_norm[None, :]

    x2 = x.reshape(bsz * s_len, d)
    for layer in range(depth):
        qk, vt, u, c = _in_proj(x2, an, wqk, wvt, wu, wc, layer)
        oa = _attn_a(qk, vt, lamvec, gain, layer, bsz, s_len)
        obc = _mix_bc(u, c, rbp, pw, ps, layer, bsz, s_len)
        x2 = _post(x2, oa, obc, wo, fn, wa, wg, cw, cb, wd, gf, layer, s_len,
                   final_norm=(layer == depth - 1))
    return x2.reshape(bsz, s_len, d)
```

```python
import functools
import math

import jax
import jax.numpy as jnp
from jax import lax
from jax.experimental import pallas as pl
from jax.experimental.pallas import tpu as pltpu

F32 = jnp.float32
BF16 = jnp.bfloat16

CHUNK = 64
HEAD_DIM = 64
A_HEADS = 4
A_V = 2 * HEAD_DIM
A_WIDTH = A_HEADS * A_V
B_GROUPS = 4
B_GROUP_DIM = 64
B_WIDTH = B_GROUPS * B_GROUP_DIM
POOL_WINDOWS = (2, 4, 8, 16)
C_HEADS = 4
C_WIDTH = C_HEADS * HEAD_DIM
C_LEFT_CHUNKS = 8
REL_CLIP = 256
EPS = 1e-5
NEG = -1e30
LOG2E = math.log2(math.e)
SUM_ROWS = 16

TM_PROJ = 512
TQ_A = 1024
TK_A = 512
STRIP_A = 256
TQ_C = 256
TM_POST = 512
FF_CHUNK = 256
POOL_HALO = 16
VMEM_LIMIT = 56 * 1024 * 1024


def _rms(x, g):
    ms = jnp.mean(x * x, axis=-1, keepdims=True)
    return x * lax.rsqrt(ms + EPS) * g


def _dot_nt(a, b):
    return lax.dot_general(a, b, (((1,), (1,)), ((), ())), preferred_element_type=F32)


def _in_proj_kernel(x_ref, g_ref, wqk_ref, wvt_ref, wu_ref, wc_ref,
                    qk_ref, vt_ref, u_ref, c_ref, *, tk):
    h = _rms(x_ref[...], g_ref[...]).astype(BF16)
    qk = jnp.dot(h, wqk_ref[...], preferred_element_type=F32)
    col = lax.broadcasted_iota(jnp.int32, (1, qk.shape[1]), 1)
    qk_ref[...] = (qk * jnp.where(col < A_WIDTH, HEAD_DIM ** -0.5 * LOG2E, 1.0)).astype(BF16)
    vt = _dot_nt(wvt_ref[...], h).astype(BF16)
    rows = A_V + SUM_ROWS
    for t in range(vt_ref.shape[0]):
        for hd in range(A_HEADS):
            vt_ref[t, hd * rows:hd * rows + A_V, :] = vt[hd * A_V:(hd + 1) * A_V, t * tk:(t + 1) * tk]
            vt_ref[t, hd * rows + A_V:(hd + 1) * rows, :] = jnp.ones((SUM_ROWS, tk), BF16)
    u_ref[...] = jnp.dot(h, wu_ref[...], preferred_element_type=F32)
    c = jnp.dot(h, wc_ref[...], preferred_element_type=F32)
    colc = lax.broadcasted_iota(jnp.int32, (1, c.shape[1]), 1)
    c_ref[...] = (c * jnp.where(colc < C_WIDTH, HEAD_DIM ** -0.5, 1.0)).astype(BF16)


def _in_proj(x2, attn_norm, wqk, wvt, wu, wc, layer):
    n, d = x2.shape
    tm, tk = TM_PROJ, TK_A
    lsel = lambda *_: (layer, 0, 0)
    return pl.pallas_call(
        functools.partial(_in_proj_kernel, tk=tk),
        grid=(n // tm,),
        in_specs=[
            pl.BlockSpec((tm, d), lambda i: (i, 0)),
            pl.BlockSpec((None, 1, d), lsel),
            pl.BlockSpec((None,) + wqk.shape[1:], lsel),
            pl.BlockSpec((None,) + wvt.shape[1:], lsel),
            pl.BlockSpec((None,) + wu.shape[1:], lsel),
            pl.BlockSpec((None,) + wc.shape[1:], lsel),
        ],
        out_specs=[
            pl.BlockSpec((tm, 2 * A_WIDTH), lambda i: (i, 0)),
            pl.BlockSpec((tm // tk, A_HEADS * (A_V + SUM_ROWS), tk), lambda i: (i, 0, 0)),
            pl.BlockSpec((tm, B_WIDTH), lambda i: (i, 0)),
            pl.BlockSpec((tm, 3 * C_WIDTH), lambda i: (i, 0)),
        ],
        out_shape=[
            jax.ShapeDtypeStruct((n, 2 * A_WIDTH), BF16),
            jax.ShapeDtypeStruct((n // tk, A_HEADS * (A_V + SUM_ROWS), tk), BF16),
            jax.ShapeDtypeStruct((n, B_WIDTH), F32),
            jax.ShapeDtypeStruct((n, 3 * C_WIDTH), BF16),
        ],
        compiler_params=pltpu.CompilerParams(
            dimension_semantics=("arbitrary",), vmem_limit_bytes=VMEM_LIMIT),
        name="in_proj",
    )(x2, attn_norm, wqk, wvt, wu, wc)


def _attn_a_kernel(lam_ref, gain_ref, q_ref, k_ref, vt_ref, o_ref,
                   qcat_ref, sa_ref, sb_ref, m_ref, l_ref, acc_ref,
                   *, tq, tk, strip, lam_init):
    i = pl.program_id(2)
    q = q_ref[...]
    lane = lax.broadcasted_iota(jnp.int32, q.shape, 1)
    zero = jnp.zeros_like(q)
    qcat_ref[0:tq, :] = jnp.where(lane < HEAD_DIM, q, zero)
    qcat_ref[tq:2 * tq, :] = jnp.where(lane >= HEAD_DIM, q, zero)
    m_ref[...] = jnp.full(m_ref.shape, NEG, F32)
    l_ref[...] = jnp.zeros(l_ref.shape, F32)
    acc_ref[...] = jnp.zeros(acc_ref.shape, F32)
    nstrip = 2 * tq // strip

    def scores(j, c, dst_ref):
        cs = slice(c * strip, (c + 1) * strip)
        k_t = k_ref[pl.ds(pl.multiple_of(j * tk, tk), tk), :]
        dst_ref[:, cs] = _dot_nt(k_t, qcat_ref[cs, :])

    def visible(c, diag):
        qblock = ((c * strip) % tq) // tk
        return 2 if diag is None or qblock > diag else int(qblock == diag)

    def consume(t, c, src_ref, diag=None):
        if visible(c, diag) == 0:
            return
        cs = slice(c * strip, (c + 1) * strip)
        s = src_ref[:, cs]
        if visible(c, diag) == 1:
            kc = lax.broadcasted_iota(jnp.int32, s.shape, 0) // CHUNK
            qc = ((c * strip) % tk + lax.broadcasted_iota(jnp.int32, s.shape, 1)) // CHUNK
            s = jnp.where(kc <= qc, s, NEG)
        m_old = m_ref[:, cs]
        m_new = jnp.maximum(m_old, jnp.max(s, axis=0, keepdims=True))
        alpha = jnp.exp2(m_old - m_new)
        p = jnp.exp2(s - m_new).astype(BF16)
        pv = jnp.dot(vt_ref[t], p, preferred_element_type=F32)
        l_ref[:, cs] = alpha * l_ref[:, cs] + pv[A_V:A_V + 1, :]
        acc_ref[:, cs] = alpha * acc_ref[:, cs] + pv[:A_V, :]
        m_ref[:, cs] = m_new

    def pipelined(j, dst_ref, src_ref, diag_next=None, diag=None):
        for c in range(nstrip):
            if visible(c, diag_next):
                scores(j, c, dst_ref)
            consume(j - 1, c, src_ref, diag)

    for c in range(nstrip):
        scores(0, c, sa_ref)

    def pair(mm, carry):
        pipelined(2 * mm - 1, sb_ref, sa_ref)
        pipelined(2 * mm, sa_ref, sb_ref)
        return carry

    lax.fori_loop(1, i + 1, pair, 0)
    pipelined(2 * i + 1, sb_ref, sa_ref, diag_next=1, diag=0)
    for c in range(nstrip):
        consume(2 * i + 1, c, sb_ref, diag=1)

    o = acc_ref[...] / l_ref[...]
    lv = lam_ref[...]
    lam = (jnp.exp(jnp.sum(lv[0:1] * lv[1:2], axis=1, keepdims=True))
           - jnp.exp(jnp.sum(lv[2:3] * lv[3:4], axis=1, keepdims=True)) + lam_init)
    a = o[:, :tq] - lam * o[:, tq:]
    ms = jnp.mean(a * a, axis=0, keepdims=True)
    y = a * lax.rsqrt(ms + EPS) * gain_ref[...] * (1.0 - lam_init)
    o_ref[...] = y.T.astype(o_ref.dtype)


def _attn_a(qk, vt, lamvec, gain, layer, bsz, s_len):
    n = qk.shape[0]
    tq, tk = TQ_A, TK_A
    assert tq == 2 * tk and tk % STRIP_A == 0
    nq = s_len // tq
    lam_init = 0.8 - 0.6 * math.exp(-0.3 * layer)
    return pl.pallas_call(
        functools.partial(_attn_a_kernel, tq=tq, tk=tk, strip=STRIP_A, lam_init=lam_init),
        grid=(bsz, A_HEADS, nq),
        in_specs=[
            pl.BlockSpec((None, 4, HEAD_DIM), lambda b, h, i: (layer, 0, 0)),
            pl.BlockSpec((None, A_V, 1), lambda b, h, i: (layer, 0, 0)),
            pl.BlockSpec((tq, A_V), lambda b, h, i: (b * nq + i, h)),
            pl.BlockSpec((s_len, A_V), lambda b, h, i: (b, A_HEADS + h)),
            pl.BlockSpec((s_len // tk, A_V + SUM_ROWS, tk), lambda b, h, i: (b, h, 0)),
        ],
        out_specs=pl.BlockSpec((tq, A_V), lambda b, h, i: (b * nq + i, h)),
        out_shape=jax.ShapeDtypeStruct((n, A_WIDTH), BF16),
        scratch_shapes=[
            pltpu.VMEM((2 * tq, A_V), BF16),
            pltpu.VMEM((tk, 2 * tq), F32),
            pltpu.VMEM((tk, 2 * tq), F32),
            pltpu.VMEM((1, 2 * tq), F32),
            pltpu.VMEM((1, 2 * tq), F32),
            pltpu.VMEM((A_V, 2 * tq), F32),
        ],
        compiler_params=pltpu.CompilerParams(
            dimension_semantics=("arbitrary", "arbitrary", "arbitrary"),
            vmem_limit_bytes=VMEM_LIMIT),
        name="attn_a",
    )(lamvec, gain, qk, qk, vt)


def _mix_bc_kernel(rb_ref, pw_ref, ps_ref, u_ref, up_ref, q_ref,
                   k0_ref, k1_ref, k2_ref, v0_ref, v1_ref, v2_ref,
                   o_ref, bias_ref, *, tq):
    j = pl.program_id(1)
    nk = 3 * tq

    @pl.when((pl.program_id(0) == 0) & (j == 0))
    def _build_bias():
        rb = rb_ref[...]
        lane = lax.broadcasted_iota(jnp.int32, rb.shape, 1)
        edge = rb[:, 2 * REL_CLIP:2 * REL_CLIP + 1]
        ext = jnp.where(lane > 2 * REL_CLIP, edge, rb)
        qc = lax.broadcasted_iota(jnp.int32, (tq, nk), 0) // CHUNK
        kc = lax.broadcasted_iota(jnp.int32, (tq, nk), 1) // CHUNK
        band = (kc >= qc) & (kc <= qc + C_LEFT_CHUNKS)
        for h in range(C_HEADS):
            rows = jnp.broadcast_to(ext[h:h + 1, :], (nk, ext.shape[1]))
            rolled = pltpu.roll(rows, REL_CLIP, axis=1, stride=1, stride_axis=0)
            bias_ref[h] = jnp.where(band, rolled[:, :tq].T, NEG)

    u = u_ref[...]
    halo = jnp.where(j > 0, up_ref[...], 0.0)
    ext_u = jnp.concatenate([halo, u], axis=0)
    s2 = ext_u + pltpu.roll(ext_u, 1, axis=0)
    s4 = s2 + pltpu.roll(s2, 2, axis=0)
    s8 = s4 + pltpu.roll(s4, 4, axis=0)
    s16 = s8 + pltpu.roll(s8, 8, axis=0)
    grp = lax.broadcasted_iota(jnp.int32, (1, B_WIDTH), 1) // B_GROUP_DIM
    wsum = jnp.where(grp == 0, s2, jnp.where(grp == 1, s4, jnp.where(grp == 2, s8, s16)))
    wsum = wsum[POOL_HALO:, :]
    win = jnp.where(grp == 0, 2, jnp.where(grp == 1, 4, jnp.where(grp == 2, 8, 16)))
    pos = j * tq + lax.broadcasted_iota(jnp.int32, (tq, 1), 0)
    cnt = jnp.minimum(pos + 1, win).astype(F32)
    d = wsum / cnt - u
    yb = jnp.dot(d.astype(BF16), pw_ref[...], preferred_element_type=F32) * ps_ref[...]
    o_ref[:, 0:B_WIDTH] = yb.astype(o_ref.dtype)

    q = q_ref[...]
    kk = jnp.concatenate([k0_ref[...], k1_ref[...], k2_ref[...]], axis=0)
    vv = jnp.concatenate([v0_ref[...], v1_ref[...], v2_ref[...]], axis=0)
    kcol = lax.broadcasted_iota(jnp.int32, (1, nk), 1)
    first_valid = jnp.maximum(2 - j, 0) * tq
    kneg = jnp.where(kcol >= first_valid, 0.0, NEG)
    head = lax.broadcasted_iota(jnp.int32, (1, C_WIDTH), 1) // HEAD_DIM
    out = jnp.zeros((tq, C_WIDTH), F32)
    for h in range(C_HEADS):
        qh = jnp.where(head == h, q, jnp.zeros_like(q))
        s = _dot_nt(qh, kk) + bias_ref[h] + kneg
        m = jnp.max(s, axis=1, keepdims=True)
        p = jnp.exp(s - m)
        l = jnp.sum(p, axis=1, keepdims=True)
        oh = jnp.dot(p.astype(BF16), vv, preferred_element_type=F32) / l
        out = jnp.where(head == h, oh, out)
    o_ref[:, B_WIDTH:B_WIDTH + C_WIDTH] = out.astype(o_ref.dtype)


def _mix_bc(u, c, rbp, pw, ps, layer, bsz, s_len):
    n = u.shape[0]
    tq = TQ_C
    assert 2 * tq == C_LEFT_CHUNKS * CHUNK and rbp.shape[-1] == 4 * tq
    nt = s_len // tq
    hpt = tq // POOL_HALO
    cur = lambda col: (lambda b, j: (b * nt + j, col))
    prev = lambda back, col: (lambda b, j: (b * nt + jnp.maximum(j - back, 0), col))
    blk = lambda: (tq, C_WIDTH)
    return pl.pallas_call(
        functools.partial(_mix_bc_kernel, tq=tq),
        grid=(bsz, nt),
        in_specs=[
            pl.BlockSpec((None,) + rbp.shape[1:], lambda b, j: (layer, 0, 0)),
            pl.BlockSpec((None,) + pw.shape[1:], lambda b, j: (layer, 0, 0)),
            pl.BlockSpec((None, 1, B_WIDTH), lambda b, j: (layer, 0, 0)),
            pl.BlockSpec((tq, B_WIDTH), cur(0)),
            pl.BlockSpec((POOL_HALO, B_WIDTH),
                         lambda b, j: ((b * nt + j) * hpt - jnp.minimum(j, 1), 0)),
            pl.BlockSpec(blk(), cur(0)),
            pl.BlockSpec(blk(), prev(2, 1)), pl.BlockSpec(blk(), prev(1, 1)), pl.BlockSpec(blk(), cur(1)),
            pl.BlockSpec(blk(), prev(2, 2)), pl.BlockSpec(blk(), prev(1, 2)), pl.BlockSpec(blk(), cur(2)),
        ],
        out_specs=pl.BlockSpec((tq, B_WIDTH + C_WIDTH), lambda b, j: (b * nt + j, 0)),
        out_shape=jax.ShapeDtypeStruct((n, B_WIDTH + C_WIDTH), BF16),
        scratch_shapes=[pltpu.VMEM((C_HEADS, tq, 3 * tq), F32)],
        compiler_params=pltpu.CompilerParams(
            dimension_semantics=("arbitrary", "arbitrary"), vmem_limit_bytes=VMEM_LIMIT),
        name="mix_bc",
    )(rbp, pw, ps, u, u, c, c, c, c, c, c, c)


def _post_kernel(x_ref, a_ref, bc_ref, wo_ref, g_ref, wa_ref, wg_ref, cw_ref, cb_ref, wd_ref, gf_ref,
                 o_ref, h_ref, acc_ref, carry_ref, a0_ref, g0_ref, a1_ref, g1_ref, hid_ref,
                 *, tiles_per_seq, final_norm):
    i = pl.program_id(0)
    tm = x_ref.shape[0]
    xm = (x_ref[...]
          + jnp.dot(a_ref[...], wo_ref[0:A_WIDTH, :], preferred_element_type=F32)
          + jnp.dot(bc_ref[...], wo_ref[A_WIDTH:, :], preferred_element_type=F32))
    h_ref[...] = _rms(xm, g_ref[...]).astype(BF16)
    acc_ref[...] = xm
    seq_start = (i % tiles_per_seq) == 0
    row = lax.broadcasted_iota(jnp.int32, (tm, 1), 0)

    def up(c, a_ref, g_ref):
        h = h_ref[...]
        a_ref[...] = jnp.dot(h, wa_ref[c], preferred_element_type=F32)
        g_ref[...] = jnp.dot(h, wg_ref[c], preferred_element_type=F32)

    def gate(c, a_ref, g_ref):
        a = a_ref[...]
        tail = jnp.where(seq_start, 0.0, carry_ref[c])
        carry_ref[c] = a[tm - 8:, :]
        a1 = jnp.where(row == 0, tail[7:8, :], pltpu.roll(a, 1, axis=0))
        a2 = jnp.where(row == 0, tail[6:7, :],
                       jnp.where(row == 1, tail[7:8, :], pltpu.roll(a, 2, axis=0)))
        cw = cw_ref[c]
        conv = cb_ref[c] + cw[0:1, :] * a2
        conv = conv + cw[1:2, :] * a1
        conv = conv + cw[2:3, :] * a
        hid_ref[c] = (conv * jax.nn.sigmoid(conv) * g_ref[...]).astype(BF16)

    nch = wa_ref.shape[0]
    assert nch % 2 == 1
    up(0, a0_ref, g0_ref)

    def pair(pp, carry):
        c = 2 * pp
        up(c + 1, a1_ref, g1_ref)
        gate(c, a0_ref, g0_ref)
        up(c + 2, a0_ref, g0_ref)
        gate(c + 1, a1_ref, g1_ref)
        return carry

    lax.fori_loop(0, nch // 2, pair, 0)
    gate(nch - 1, a0_ref, g0_ref)
    out = acc_ref[...]
    for c in range(nch):
        out = out + jnp.dot(hid_ref[c], wd_ref[c], preferred_element_type=F32)
    if final_norm:
        out = _rms(out, gf_ref[...])
    o_ref[...] = out


def _post(x2, oa, obc, wo, ffn_norm, wa, wg, cw, cb, wd, gf, layer, s_len, final_norm):
    n, d = x2.shape
    tm = TM_POST
    nch, _, fc = wa.shape[1:]
    const = pl.Buffered(1)
    l3 = lambda i: (layer, 0, 0)
    l4 = lambda i: (layer, 0, 0, 0)
    return pl.pallas_call(
        functools.partial(_post_kernel, tiles_per_seq=s_len // tm, final_norm=final_norm),
        grid=(n // tm,),
        in_specs=[
            pl.BlockSpec((tm, d), lambda i: (i, 0)),
            pl.BlockSpec((tm, A_WIDTH), lambda i: (i, 0)),
            pl.BlockSpec((tm, B_WIDTH + C_WIDTH), lambda i: (i, 0)),
            pl.BlockSpec((None,) + wo.shape[1:], l3, pipeline_mode=const),
            pl.BlockSpec((None, 1, d), l3),
            pl.BlockSpec((None,) + wa.shape[1:], l4, pipeline_mode=const),
            pl.BlockSpec((None,) + wg.shape[1:], l4, pipeline_mode=const),
            pl.BlockSpec((None,) + cw.shape[1:], l4),
            pl.BlockSpec((None,) + cb.shape[1:], l4),
            pl.BlockSpec((None,) + wd.shape[1:], l4, pipeline_mode=const),
            pl.BlockSpec((1, d), lambda i: (0, 0)),
        ],
        out_specs=pl.BlockSpec((tm, d), lambda i: (i, 0)),
        out_shape=jax.ShapeDtypeStruct((n, d), F32),
        scratch_shapes=[
            pltpu.VMEM((tm, d), BF16),
            pltpu.VMEM((tm, d), F32),
            pltpu.VMEM((nch, 8, fc), F32),
            pltpu.VMEM((tm, fc), F32), pltpu.VMEM((tm, fc), F32),
            pltpu.VMEM((tm, fc), F32), pltpu.VMEM((tm, fc), F32),
            pltpu.VMEM((nch, tm, fc), BF16),
        ],
        compiler_params=pltpu.CompilerParams(
            dimension_semantics=("arbitrary",), vmem_limit_bytes=VMEM_LIMIT),
        name="post",
    )(x2, oa, obc, wo, ffn_norm, wa, wg, cw, cb, wd, gf)


def kernel(x, attn_norm, w_in, lam_q1, lam_k1, lam_q2, lam_k2, diff_subln, pool_w, pool_scale,
           rel_bias, w_out, ffn_norm, w_up, conv_w, conv_b, w_down, final_norm):
    bsz, s_len, d = x.shape
    depth = w_in.shape[0]
    d_ff = w_down.shape[1]
    nch = d_ff // FF_CHUNK
    assert nch * FF_CHUNK == d_ff and s_len % TQ_A == 0 and s_len % TM_POST == 0

    w_in_b = w_in.astype(BF16)
    wqk = w_in_b[:, :, :2 * A_WIDTH]
    wvt = jnp.swapaxes(w_in_b[:, :, 2 * A_WIDTH:3 * A_WIDTH], 1, 2)
    wu = w_in_b[:, :, 3 * A_WIDTH:3 * A_WIDTH + B_WIDTH]
    wc = w_in_b[:, :, 3 * A_WIDTH + B_WIDTH:]
    wo = w_out.astype(BF16)
    w_up_b = w_up.astype(BF16)
    wa = w_up_b[:, :, :d_ff].reshape(depth, d, nch, FF_CHUNK).transpose(0, 2, 1, 3)
    wg = w_up_b[:, :, d_ff:].reshape(depth, d, nch, FF_CHUNK).transpose(0, 2, 1, 3)
    wd = w_down.astype(BF16).reshape(depth, nch, FF_CHUNK, d)
    cw = conv_w.reshape(depth, conv_w.shape[1], nch, FF_CHUNK).transpose(0, 2, 1, 3)
    cb = conv_b.reshape(depth, nch, 1, FF_CHUNK)
    lamvec = jnp.stack([lam_q1, lam_k1, lam_q2, lam_k2], axis=1)
    gain = diff_subln[:, :, None]
    eye = jnp.eye(B_GROUPS, dtype=pool_w.dtype)
    pw = jnp.einsum('lgcd,gh->lgchd', pool_w, eye).reshape(depth, B_WIDTH, B_WIDTH).astype(BF16)
    ps = pool_scale[:, None, :]
    rbp = jnp.pad(rel_bias, ((0, 0), (0, 0), (0, 4 * TQ_C - rel_bias.shape[-1])))
    an = attn_norm[:, None, :]
    fn = ffn_norm[:, None, :]
    gf = final_norm[None, :]

    x2 = x.reshape(bsz * s_len, d)
    for layer in range(depth):
        qk, vt, u, c = _in_proj(x2, an, wqk, wvt, wu, wc, layer)
        oa = _attn_a(qk, vt, lamvec, gain, layer, bsz, s_len)
        obc = _mix_bc(u, c, rbp, pw, ps, layer, bsz, s_len)
        x2 = _post(x2, oa, obc, wo, fn, wa, wg, cw, cb, wd, gf, layer, s_len,
                   final_norm=(layer == depth - 1))
    return x2.reshape(bsz, s_len, d)
```

```python
import functools
import math

import jax
import jax.numpy as jnp
from jax import lax
from jax.experimental import pallas as pl
from jax.experimental.pallas import tpu as pltpu

F32 = jnp.float32
BF16 = jnp.bfloat16

CHUNK = 64
HEAD_DIM = 64
A_HEADS = 4
A_V = 2 * HEAD_DIM
A_WIDTH = A_HEADS * A_V
B_GROUPS = 4
B_GROUP_DIM = 64
B_WIDTH = B_GROUPS * B_GROUP_DIM
POOL_WINDOWS = (2, 4, 8, 16)
C_HEADS = 4
C_WIDTH = C_HEADS * HEAD_DIM
C_LEFT_CHUNKS = 8
REL_CLIP = 256
EPS = 1e-5
NEG = -1e30
LOG2E = math.log2(math.e)
SUM_ROWS = 16

TM_PROJ = 512
TQ_A = 1024
TK_A = 512
STRIP_A = 256
TQ_C = 256
TM_POST = 512
FF_CHUNK = 256
POOL_HALO = 16
VMEM_LIMIT = 56 * 1024 * 1024


def _rms(x, g):
    ms = jnp.mean(x * x, axis=-1, keepdims=True)
    return x * lax.rsqrt(ms + EPS) * g


def _dot_nt(a, b):
    return lax.dot_general(a, b, (((1,), (1,)), ((), ())), preferred_element_type=F32)


def _in_proj_kernel(x_ref, g_ref, w_ref, wvt_ref, qk_ref, vt_ref, u_ref, c_ref, *, tk):
    u0 = 3 * A_WIDTH
    c0 = u0 + B_WIDTH
    h = _rms(x_ref[...], g_ref[...]).astype(BF16)
    qk = jnp.dot(h, w_ref[:, :2 * A_WIDTH], preferred_element_type=F32)
    col = lax.broadcasted_iota(jnp.int32, (1, qk.shape[1]), 1)
    qk_ref[...] = (qk * jnp.where(col < A_WIDTH, HEAD_DIM ** -0.5 * LOG2E, 1.0)).astype(BF16)
    vt = _dot_nt(wvt_ref[...], h).astype(BF16)
    rows = A_V + SUM_ROWS
    for t in range(vt_ref.shape[0]):
        for hd in range(A_HEADS):
            vt_ref[t, hd * rows:hd * rows + A_V, :] = vt[hd * A_V:(hd + 1) * A_V, t * tk:(t + 1) * tk]
            vt_ref[t, hd * rows + A_V:(hd + 1) * rows, :] = jnp.ones((SUM_ROWS, tk), BF16)
    u_ref[...] = jnp.dot(h, w_ref[:, u0:c0], preferred_element_type=F32)
    c = jnp.dot(h, w_ref[:, c0:], preferred_element_type=F32)
    colc = lax.broadcasted_iota(jnp.int32, (1, c.shape[1]), 1)
    c_ref[...] = (c * jnp.where(colc < C_WIDTH, HEAD_DIM ** -0.5 * LOG2E, 1.0)).astype(BF16)


def _in_proj(x2, attn_norm, w_in_b, wvt, layer):
    n, d = x2.shape
    tm, tk = TM_PROJ, TK_A
    lsel = lambda *_: (layer, 0, 0)
    return pl.pallas_call(
        functools.partial(_in_proj_kernel, tk=tk),
        grid=(n // tm,),
        in_specs=[
            pl.BlockSpec((tm, d), lambda i: (i, 0)),
            pl.BlockSpec((None, 1, d), lsel),
            pl.BlockSpec((None,) + w_in_b.shape[1:], lsel),
            pl.BlockSpec((None,) + wvt.shape[1:], lsel),
        ],
        out_specs=[
            pl.BlockSpec((tm, 2 * A_WIDTH), lambda i: (i, 0)),
            pl.BlockSpec((tm // tk, A_HEADS * (A_V + SUM_ROWS), tk), lambda i: (i, 0, 0)),
            pl.BlockSpec((tm, B_WIDTH), lambda i: (i, 0)),
            pl.BlockSpec((tm, 3 * C_WIDTH), lambda i: (i, 0)),
        ],
        out_shape=[
            jax.ShapeDtypeStruct((n, 2 * A_WIDTH), BF16),
            jax.ShapeDtypeStruct((n // tk, A_HEADS * (A_V + SUM_ROWS), tk), BF16),
            jax.ShapeDtypeStruct((n, B_WIDTH), F32),
            jax.ShapeDtypeStruct((n, 3 * C_WIDTH), BF16),
        ],
        compiler_params=pltpu.CompilerParams(
            dimension_semantics=("arbitrary",), vmem_limit_bytes=VMEM_LIMIT),
        name="in_proj",
    )(x2, attn_norm, w_in_b, wvt)


def _attn_a_kernel(lam_ref, gain_ref, q_ref, k_ref, vt_ref, o_ref,
                   qcat_ref, sa_ref, sb_ref, m_ref, l_ref, acc_ref,
                   *, tq, tk, strip, nq, lam_init):
    nstrip = 2 * tq // strip

    def load_queries(i):
        q = q_ref[pl.ds(pl.multiple_of(i * tq, tq), tq), :]
        lane = lax.broadcasted_iota(jnp.int32, q.shape, 1)
        zero = jnp.zeros_like(q)
        qcat_ref[0:tq, :] = jnp.where(lane < HEAD_DIM, q, zero)
        qcat_ref[tq:2 * tq, :] = jnp.where(lane >= HEAD_DIM, q, zero)

    def reset_state():
        m_ref[...] = jnp.full(m_ref.shape, NEG, F32)
        l_ref[...] = jnp.zeros(l_ref.shape, F32)
        acc_ref[...] = jnp.zeros(acc_ref.shape, F32)

    def scores(j, c, dst_ref):
        cs = slice(c * strip, (c + 1) * strip)
        k_t = k_ref[pl.ds(pl.multiple_of(j * tk, tk), tk), :]
        dst_ref[:, cs] = _dot_nt(k_t, qcat_ref[cs, :])

    def visible(c, diag):
        qblock = ((c * strip) % tq) // tk
        return 2 if diag is None or qblock > diag else int(qblock == diag)

    def consume(t, c, src_ref, diag=None):
        if visible(c, diag) == 0:
            return
        cs = slice(c * strip, (c + 1) * strip)
        s = src_ref[:, cs]
        if visible(c, diag) == 1:
            kc = lax.broadcasted_iota(jnp.int32, s.shape, 0) // CHUNK
            qc = ((c * strip) % tk + lax.broadcasted_iota(jnp.int32, s.shape, 1)) // CHUNK
            s = jnp.where(kc <= qc, s, NEG)
        m_old = m_ref[:, cs]
        m_new = jnp.maximum(m_old, jnp.max(s, axis=0, keepdims=True))
        alpha = jnp.exp2(m_old - m_new)
        p = jnp.exp2(s - m_new).astype(BF16)
        pv = jnp.dot(vt_ref[t], p, preferred_element_type=F32)
        l_ref[:, cs] = alpha * l_ref[:, cs] + pv[A_V:A_V + 1, :]
        acc_ref[:, cs] = alpha * acc_ref[:, cs] + pv[:A_V, :]
        m_ref[:, cs] = m_new

    def pipelined(j, dst_ref, src_ref, diag_next=None, diag=None):
        for c in range(nstrip):
            if visible(c, diag_next):
                scores(j, c, dst_ref)
            consume(j - 1, c, src_ref, diag)

    def finalize(i):
        o = acc_ref[...] / l_ref[...]
        lv = lam_ref[...]
        lam = (jnp.exp(jnp.sum(lv[0:1] * lv[1:2], axis=1, keepdims=True))
               - jnp.exp(jnp.sum(lv[2:3] * lv[3:4], axis=1, keepdims=True)) + lam_init)
        a = o[:, :tq] - lam * o[:, tq:]
        ms = jnp.mean(a * a, axis=0, keepdims=True)
        y = a * lax.rsqrt(ms + EPS) * gain_ref[...] * (1.0 - lam_init)
        o_ref[pl.ds(pl.multiple_of(i * tq, tq), tq), :] = y.T.astype(o_ref.dtype)

    def pair(mm, carry):
        pipelined(2 * mm - 1, sb_ref, sa_ref)
        pipelined(2 * mm, sa_ref, sb_ref)
        return carry

    load_queries(0)
    reset_state()
    for c in range(nstrip):
        scores(0, c, sa_ref)

    def query_tile(i, carry):
        lax.fori_loop(1, i + 1, pair, 0)
        pipelined(2 * i + 1, sb_ref, sa_ref, diag_next=1, diag=0)
        load_queries(jnp.minimum(i + 1, nq - 1))
        for c in range(nstrip):
            scores(0, c, sa_ref)
            consume(2 * i + 1, c, sb_ref, diag=1)
        finalize(i)
        reset_state()
        return carry

    lax.fori_loop(0, nq, query_tile, 0)


def _attn_a(qk, vt, lamvec, gain, layer, bsz, s_len):
    n = qk.shape[0]
    tq, tk = TQ_A, TK_A
    assert tq == 2 * tk and tk % STRIP_A == 0
    nq = s_len // tq
    lam_init = 0.8 - 0.6 * math.exp(-0.3 * layer)
    return pl.pallas_call(
        functools.partial(_attn_a_kernel, tq=tq, tk=tk, strip=STRIP_A, nq=nq, lam_init=lam_init),
        grid=(bsz, A_HEADS),
        in_specs=[
            pl.BlockSpec((None, 4, HEAD_DIM), lambda b, h: (layer, 0, 0)),
            pl.BlockSpec((None, A_V, 1), lambda b, h: (layer, 0, 0)),
            pl.BlockSpec((s_len, A_V), lambda b, h: (b, h)),
            pl.BlockSpec((s_len, A_V), lambda b, h: (b, A_HEADS + h)),
            pl.BlockSpec((s_len // tk, A_V + SUM_ROWS, tk), lambda b, h: (b, h, 0)),
        ],
        out_specs=pl.BlockSpec((s_len, A_V), lambda b, h: (b, h)),
        out_shape=jax.ShapeDtypeStruct((n, A_WIDTH), BF16),
        scratch_shapes=[
            pltpu.VMEM((2 * tq, A_V), BF16),
            pltpu.VMEM((tk, 2 * tq), F32),
            pltpu.VMEM((tk, 2 * tq), F32),
            pltpu.VMEM((1, 2 * tq), F32),
            pltpu.VMEM((1, 2 * tq), F32),
            pltpu.VMEM((A_V, 2 * tq), F32),
        ],
        compiler_params=pltpu.CompilerParams(
            dimension_semantics=("arbitrary", "arbitrary"),
            vmem_limit_bytes=VMEM_LIMIT),
        name="attn_a",
    )(lamvec, gain, qk, qk, vt)


def _mix_bc_kernel(rb_ref, pw_ref, ps_ref, u_ref, up_ref, q_ref,
                   k0_ref, k1_ref, k2_ref, v0_ref, v1_ref, v2_ref,
                   o_ref, bias_ref, s_ref, *, tq):
    j = pl.program_id(1)
    nk = 3 * tq

    @pl.when((pl.program_id(0) == 0) & (j == 0))
    def _build_bias():
        rb = rb_ref[...]
        lane = lax.broadcasted_iota(jnp.int32, rb.shape, 1)
        edge = rb[:, 2 * REL_CLIP:2 * REL_CLIP + 1]
        ext = jnp.where(lane > 2 * REL_CLIP, edge, rb)
        qc = lax.broadcasted_iota(jnp.int32, (tq, nk), 0) // CHUNK
        kc = lax.broadcasted_iota(jnp.int32, (tq, nk), 1) // CHUNK
        band = (kc >= qc) & (kc <= qc + C_LEFT_CHUNKS)
        for h in range(C_HEADS):
            rows = jnp.broadcast_to(ext[h:h + 1, :], (nk, ext.shape[1]))
            rolled = pltpu.roll(rows, REL_CLIP, axis=1, stride=1, stride_axis=0)
            bias_ref[h] = jnp.where(band, rolled[:, :tq].T * LOG2E, NEG)

    u = u_ref[...]
    halo = jnp.where(j > 0, up_ref[...], 0.0)
    ext_u = jnp.concatenate([halo, u], axis=0)
    s2 = ext_u + pltpu.roll(ext_u, 1, axis=0)
    s4 = s2 + pltpu.roll(s2, 2, axis=0)
    s8 = s4 + pltpu.roll(s4, 4, axis=0)
    s16 = s8 + pltpu.roll(s8, 8, axis=0)
    grp = lax.broadcasted_iota(jnp.int32, (1, B_WIDTH), 1) // B_GROUP_DIM
    wsum = jnp.where(grp == 0, s2, jnp.where(grp == 1, s4, jnp.where(grp == 2, s8, s16)))
    wsum = wsum[POOL_HALO:, :]
    win = jnp.where(grp == 0, 2, jnp.where(grp == 1, 4, jnp.where(grp == 2, 8, 16)))
    pos = j * tq + lax.broadcasted_iota(jnp.int32, (tq, 1), 0)
    cnt = jnp.minimum(pos + 1, win).astype(F32)
    d = wsum / cnt - u
    yb = jnp.dot(d.astype(BF16), pw_ref[...], preferred_element_type=F32) * ps_ref[...]
    o_ref[:, 0:B_WIDTH] = yb.astype(o_ref.dtype)

    q = q_ref[...]
    kk = jnp.concatenate([k0_ref[...], k1_ref[...], k2_ref[...]], axis=0)
    vv = jnp.concatenate([v0_ref[...], v1_ref[...], v2_ref[...]], axis=0)
    kcol = lax.broadcasted_iota(jnp.int32, (1, nk), 1)
    first_valid = jnp.maximum(2 - j, 0) * tq
    kneg = jnp.where(kcol >= first_valid, 0.0, NEG)
    head = lax.broadcasted_iota(jnp.int32, (1, C_WIDTH), 1) // HEAD_DIM
    out = jnp.zeros((tq, C_WIDTH), F32)
    for h in range(C_HEADS):
        qh = jnp.where(head == h, q, jnp.zeros_like(q))
        s_ref[h] = _dot_nt(qh, kk)
    for h in range(C_HEADS):
        s = s_ref[h] + bias_ref[h] + kneg
        m = jnp.max(s, axis=1, keepdims=True)
        p = jnp.exp2(s - m)
        l = jnp.sum(p, axis=1, keepdims=True)
        oh = jnp.dot(p.astype(BF16), vv, preferred_element_type=F32) / l
        out = jnp.where(head == h, oh, out)
    o_ref[:, B_WIDTH:B_WIDTH + C_WIDTH] = out.astype(o_ref.dtype)


def _mix_bc(u, c, rbp, pw, ps, layer, bsz, s_len):
    n = u.shape[0]
    tq = TQ_C
    assert 2 * tq == C_LEFT_CHUNKS * CHUNK and rbp.shape[-1] == 4 * tq
    nt = s_len // tq
    hpt = tq // POOL_HALO
    cur = lambda col: (lambda b, j: (b * nt + j, col))
    prev = lambda back, col: (lambda b, j: (b * nt + jnp.maximum(j - back, 0), col))
    blk = lambda: (tq, C_WIDTH)
    return pl.pallas_call(
        functools.partial(_mix_bc_kernel, tq=tq),
        grid=(bsz, nt),
        in_specs=[
            pl.BlockSpec((None,) + rbp.shape[1:], lambda b, j: (layer, 0, 0)),
            pl.BlockSpec((None,) + pw.shape[1:], lambda b, j: (layer, 0, 0)),
            pl.BlockSpec((None, 1, B_WIDTH), lambda b, j: (layer, 0, 0)),
            pl.BlockSpec((tq, B_WIDTH), cur(0)),
            pl.BlockSpec((POOL_HALO, B_WIDTH),
                         lambda b, j: ((b * nt + j) * hpt - jnp.minimum(j, 1), 0)),
            pl.BlockSpec(blk(), cur(0)),
            pl.BlockSpec(blk(), prev(2, 1)), pl.BlockSpec(blk(), prev(1, 1)), pl.BlockSpec(blk(), cur(1)),
            pl.BlockSpec(blk(), prev(2, 2)), pl.BlockSpec(blk(), prev(1, 2)), pl.BlockSpec(blk(), cur(2)),
        ],
        out_specs=pl.BlockSpec((tq, B_WIDTH + C_WIDTH), lambda b, j: (b * nt + j, 0)),
        out_shape=jax.ShapeDtypeStruct((n, B_WIDTH + C_WIDTH), BF16),
        scratch_shapes=[pltpu.VMEM((C_HEADS, tq, 3 * tq), F32),
                        pltpu.VMEM((C_HEADS, tq, 3 * tq), F32)],
        compiler_params=pltpu.CompilerParams(
            dimension_semantics=("arbitrary", "arbitrary"), vmem_limit_bytes=VMEM_LIMIT),
        name="mix_bc",
    )(rbp, pw, ps, u, u, c, c, c, c, c, c, c)


def _post_kernel(x_ref, a_ref, bc_ref, wo_ref, g_ref, wup_ref, cw_ref, cb_ref, wd_ref, gf_ref,
                 o_ref, h_ref, acc_ref, carry_ref, a0_ref, g0_ref, a1_ref, g1_ref, hid_ref,
                 *, tiles_per_seq, final_norm):
    i = pl.program_id(0)
    tm = x_ref.shape[0]
    d_ff = wd_ref.shape[0]
    fc = a0_ref.shape[1]
    nch = d_ff // fc

    def cols(c, base=0):
        start = base + c * fc
        return pl.ds(start if isinstance(start, int) else pl.multiple_of(start, fc), fc)

    xm = (x_ref[...]
          + jnp.dot(a_ref[...], wo_ref[0:A_WIDTH, :], preferred_element_type=F32)
          + jnp.dot(bc_ref[...], wo_ref[A_WIDTH:, :], preferred_element_type=F32))
    h_ref[...] = _rms(xm, g_ref[...]).astype(BF16)
    acc_ref[...] = xm
    seq_start = (i % tiles_per_seq) == 0
    row = lax.broadcasted_iota(jnp.int32, (8, 1), 0)

    def up(c, a_ref, g_ref):
        h = h_ref[...]
        a_ref[...] = jnp.dot(h, wup_ref[:, cols(c)], preferred_element_type=F32)
        g_ref[...] = jnp.dot(h, wup_ref[:, cols(c, d_ff)], preferred_element_type=F32)

    def params(c):
        tail = jnp.where(seq_start, 0.0, carry_ref[c])
        return tail, cw_ref[:, cols(c)], cb_ref[:, cols(c)]

    def gate(c, a_ref, g_ref, prm):
        tail, cw, cb = prm
        a = a_ref[...]
        carry_ref[c] = a[tm - 8:, :]
        r1 = pltpu.roll(a, 1, axis=0)
        r2 = pltpu.roll(a, 2, axis=0)
        top1 = jnp.where(row == 0, tail[7:8, :], r1[:8, :])
        top2 = jnp.where(row == 0, tail[6:7, :], jnp.where(row == 1, tail[7:8, :], r2[:8, :]))
        a1 = jnp.concatenate([top1, r1[8:, :]], axis=0)
        a2 = jnp.concatenate([top2, r2[8:, :]], axis=0)
        conv = cb + cw[0:1, :] * a2
        conv = conv + cw[1:2, :] * a1
        conv = conv + cw[2:3, :] * a
        hid_ref[:, cols(c)] = (conv * jax.nn.sigmoid(conv) * g_ref[...]).astype(BF16)

    assert nch % 2 == 1 and nch >= 3
    up(0, a0_ref, g0_ref)
    up(1, a1_ref, g1_ref)

    def pair(pp, carry):
        c = 2 * pp
        prm0, prm1 = params(c), params(c + 1)
        gate(c, a0_ref, g0_ref, prm0)
        up(c + 2, a0_ref, g0_ref)
        gate(c + 1, a1_ref, g1_ref, prm1)
        up(c + 3, a1_ref, g1_ref)
        return carry

    lax.fori_loop(0, (nch - 3) // 2, pair, 0)
    gate(nch - 3, a0_ref, g0_ref, params(nch - 3))
    up(nch - 1, a0_ref, g0_ref)
    gate(nch - 2, a1_ref, g1_ref, params(nch - 2))
    gate(nch - 1, a0_ref, g0_ref, params(nch - 1))
    out = acc_ref[...] + jnp.dot(hid_ref[...], wd_ref[...], preferred_element_type=F32)
    if final_norm:
        out = _rms(out, gf_ref[...])
    o_ref[...] = out


def _post(x2, oa, obc, wo, ffn_norm, wup, cw, cb, wd, gf, layer, s_len, final_norm):
    n, d = x2.shape
    tm, fc = TM_POST, FF_CHUNK
    d_ff = wd.shape[1]
    nch = d_ff // fc
    const = pl.Buffered(1)
    l3 = lambda i: (layer, 0, 0)
    return pl.pallas_call(
        functools.partial(_post_kernel, tiles_per_seq=s_len // tm, final_norm=final_norm),
        grid=(n // tm,),
        in_specs=[
            pl.BlockSpec((tm, d), lambda i: (i, 0)),
            pl.BlockSpec((tm, A_WIDTH), lambda i: (i, 0)),
            pl.BlockSpec((tm, B_WIDTH + C_WIDTH), lambda i: (i, 0)),
            pl.BlockSpec((None,) + wo.shape[1:], l3, pipeline_mode=const),
            pl.BlockSpec((None, 1, d), l3),
            pl.BlockSpec((None,) + wup.shape[1:], l3, pipeline_mode=const),
            pl.BlockSpec((None,) + cw.shape[1:], l3),
            pl.BlockSpec((None,) + cb.shape[1:], l3),
            pl.BlockSpec((None,) + wd.shape[1:], l3, pipeline_mode=const),
            pl.BlockSpec((1, d), lambda i: (0, 0)),
        ],
        out_specs=pl.BlockSpec((tm, d), lambda i: (i, 0)),
        out_shape=jax.ShapeDtypeStruct((n, d), F32),
        scratch_shapes=[
            pltpu.VMEM((tm, d), BF16),
            pltpu.VMEM((tm, d), F32),
            pltpu.VMEM((nch, 8, fc), F32),
            pltpu.VMEM((tm, fc), F32), pltpu.VMEM((tm, fc), F32),
            pltpu.VMEM((tm, fc), F32), pltpu.VMEM((tm, fc), F32),
            pltpu.VMEM((tm, d_ff), BF16),
        ],
        compiler_params=pltpu.CompilerParams(
            dimension_semantics=("arbitrary",), vmem_limit_bytes=VMEM_LIMIT,
        ),
        name="post",
    )(x2, oa, obc, wo, ffn_norm, wup, cw, cb, wd, gf)


def kernel(x, attn_norm, w_in, lam_q1, lam_k1, lam_q2, lam_k2, diff_subln, pool_w, pool_scale,
           rel_bias, w_out, ffn_norm, w_up, conv_w, conv_b, w_down, final_norm):
    bsz, s_len, d = x.shape
    depth = w_in.shape[0]
    d_ff = w_down.shape[1]
    nch = d_ff // FF_CHUNK
    assert nch * FF_CHUNK == d_ff and s_len % TQ_A == 0 and s_len % TM_POST == 0

    w_in_b = w_in.astype(BF16)
    wvt = jnp.swapaxes(w_in_b[:, :, 2 * A_WIDTH:3 * A_WIDTH], 1, 2)
    wo = w_out.astype(BF16)
    wup = w_up.astype(BF16)
    wd = w_down.astype(BF16)
    cw = conv_w
    cb = conv_b[:, None, :]
    lamvec = jnp.stack([lam_q1, lam_k1, lam_q2, lam_k2], axis=1)
    gain = diff_subln[:, :, None]
    eye = jnp.eye(B_GROUPS, dtype=pool_w.dtype)
    pw = jnp.einsum('lgcd,gh->lgchd', pool_w, eye).reshape(depth, B_WIDTH, B_WIDTH).astype(BF16)
    ps = pool_scale[:, None, :]
    rbp = jnp.pad(rel_bias, ((0, 0), (0, 0), (0, 4 * TQ_C - rel_bias.shape[-1])))
    an = attn_norm[:, None, :]
    fn = ffn_norm[:, None, :]
    gf = final_norm[None, :]

    x2 = x.reshape(bsz * s_len, d)
    for layer in range(depth):
        qk, vt, u, c = _in_proj(x2, an, w_in_b, wvt, layer)
        oa = _attn_a(qk, vt, lamvec, gain, layer, bsz, s_len)
        obc = _mix_bc(u, c, rbp, pw, ps, layer, bsz, s_len)
        x2 = _post(x2, oa, obc, wo, fn, wup, cw, cb, wd, gf, layer, s_len,
                   final_norm=(layer == depth - 1))
    return x2.reshape(bsz, s_len, d)
```

```python
import functools
import math

import jax
import jax.numpy as jnp
from jax import lax
from jax.experimental import pallas as pl
from jax.experimental.pallas import tpu as pltpu

F32 = jnp.float32
BF16 = jnp.bfloat16

CHUNK = 64
HEAD_DIM = 64
A_HEADS = 4
A_V = 2 * HEAD_DIM
A_WIDTH = A_HEADS * A_V
B_GROUPS = 4
B_GROUP_DIM = 64
B_WIDTH = B_GROUPS * B_GROUP_DIM
POOL_WINDOWS = (2, 4, 8, 16)
C_HEADS = 4
C_WIDTH = C_HEADS * HEAD_DIM
C_LEFT_CHUNKS = 8
REL_CLIP = 256
EPS = 1e-5
NEG = -1e30
LOG2E = math.log2(math.e)
SUM_ROWS = 16

TM_PROJ = 512
TQ_A = 1024
TK_A = 512
STRIP_A = 512
TQ_C = 256
TM_POST = 512
FF_CHUNK = 256
POOL_HALO = 16
VMEM_LIMIT = 56 * 1024 * 1024


def _rms(x, g):
    ms = jnp.mean(x * x, axis=-1, keepdims=True)
    return x * lax.rsqrt(ms + EPS) * g


def _dot_nt(a, b):
    return lax.dot_general(a, b, (((1,), (1,)), ((), ())), preferred_element_type=F32)


def _in_proj_kernel(x_ref, g_ref, w_ref, wvt_ref, qk_ref, vt_ref, u_ref, c_ref, *, tk):
    u0 = 3 * A_WIDTH
    c0 = u0 + B_WIDTH
    h = _rms(x_ref[...], g_ref[...]).astype(BF16)
    qk = jnp.dot(h, w_ref[:, :2 * A_WIDTH], preferred_element_type=F32)
    col = lax.broadcasted_iota(jnp.int32, (1, qk.shape[1]), 1)
    qk_ref[...] = (qk * jnp.where(col < A_WIDTH, HEAD_DIM ** -0.5 * LOG2E, 1.0)).astype(BF16)
    vt = _dot_nt(wvt_ref[...], h).astype(BF16)
    rows = A_V + SUM_ROWS
    for t in range(vt_ref.shape[0]):
        for hd in range(A_HEADS):
            vt_ref[t, hd * rows:hd * rows + A_V, :] = vt[hd * A_V:(hd + 1) * A_V, t * tk:(t + 1) * tk]
            vt_ref[t, hd * rows + A_V:(hd + 1) * rows, :] = jnp.ones((SUM_ROWS, tk), BF16)
    u_ref[...] = jnp.dot(h, w_ref[:, u0:c0], preferred_element_type=F32)
    c = jnp.dot(h, w_ref[:, c0:], preferred_element_type=F32)
    colc = lax.broadcasted_iota(jnp.int32, (1, c.shape[1]), 1)
    c_ref[...] = (c * jnp.where(colc < C_WIDTH, HEAD_DIM ** -0.5 * LOG2E, 1.0)).astype(BF16)


def _in_proj(x2, attn_norm, w_in_b, wvt, layer):
    n, d = x2.shape
    tm, tk = TM_PROJ, TK_A
    lsel = lambda *_: (layer, 0, 0)
    return pl.pallas_call(
        functools.partial(_in_proj_kernel, tk=tk),
        grid=(n // tm,),
        in_specs=[
            pl.BlockSpec((tm, d), lambda i: (i, 0)),
            pl.BlockSpec((None, 1, d), lsel),
            pl.BlockSpec((None,) + w_in_b.shape[1:], lsel),
            pl.BlockSpec((None,) + wvt.shape[1:], lsel),
        ],
        out_specs=[
            pl.BlockSpec((tm, 2 * A_WIDTH), lambda i: (i, 0)),
            pl.BlockSpec((tm // tk, A_HEADS * (A_V + SUM_ROWS), tk), lambda i: (i, 0, 0)),
            pl.BlockSpec((tm, B_WIDTH), lambda i: (i, 0)),
            pl.BlockSpec((tm, 3 * C_WIDTH), lambda i: (i, 0)),
        ],
        out_shape=[
            jax.ShapeDtypeStruct((n, 2 * A_WIDTH), BF16),
            jax.ShapeDtypeStruct((n // tk, A_HEADS * (A_V + SUM_ROWS), tk), BF16),
            jax.ShapeDtypeStruct((n, B_WIDTH), F32),
            jax.ShapeDtypeStruct((n, 3 * C_WIDTH), BF16),
        ],
        compiler_params=pltpu.CompilerParams(
            dimension_semantics=("arbitrary",), vmem_limit_bytes=VMEM_LIMIT),
        name="in_proj",
    )(x2, attn_norm, w_in_b, wvt)


def _attn_a_kernel(lam_ref, gain_ref, q_ref, k_ref, vt_ref, o_ref,
                   qcat_ref, sa_ref, sb_ref, m_ref, l_ref, acc_ref,
                   *, tq, tk, strip, nq, lam_init):
    nstrip = 2 * tq // strip

    def load_queries(i):
        q = q_ref[pl.ds(pl.multiple_of(i * tq, tq), tq), :]
        lane = lax.broadcasted_iota(jnp.int32, q.shape, 1)
        zero = jnp.zeros_like(q)
        qcat_ref[0:tq, :] = jnp.where(lane < HEAD_DIM, q, zero)
        qcat_ref[tq:2 * tq, :] = jnp.where(lane >= HEAD_DIM, q, zero)

    def reset_state():
        m_ref[...] = jnp.full(m_ref.shape, NEG, F32)
        l_ref[...] = jnp.zeros(l_ref.shape, F32)
        acc_ref[...] = jnp.zeros(acc_ref.shape, F32)

    def scores(j, c, dst_ref):
        cs = slice(c * strip, (c + 1) * strip)
        k_t = k_ref[pl.ds(pl.multiple_of(j * tk, tk), tk), :]
        dst_ref[:, cs] = _dot_nt(k_t, qcat_ref[cs, :])

    def visible(c, diag):
        qblock = ((c * strip) % tq) // tk
        return 2 if diag is None or qblock > diag else int(qblock == diag)

    def consume(t, c, src_ref, diag=None):
        if visible(c, diag) == 0:
            return
        cs = slice(c * strip, (c + 1) * strip)
        s = src_ref[:, cs]
        if visible(c, diag) == 1:
            kc = lax.broadcasted_iota(jnp.int32, s.shape, 0) // CHUNK
            qc = ((c * strip) % tk + lax.broadcasted_iota(jnp.int32, s.shape, 1)) // CHUNK
            s = jnp.where(kc <= qc, s, NEG)
        m_old = m_ref[:, cs]
        m_new = jnp.maximum(m_old, jnp.max(s, axis=0, keepdims=True))
        alpha = jnp.exp2(m_old - m_new)
        p = jnp.exp2(s - m_new).astype(BF16)
        pv = jnp.dot(vt_ref[t], p, preferred_element_type=F32)
        l_ref[:, cs] = alpha * l_ref[:, cs] + pv[A_V:A_V + 1, :]
        acc_ref[:, cs] = alpha * acc_ref[:, cs] + pv[:A_V, :]
        m_ref[:, cs] = m_new

    def pipelined(j, dst_ref, src_ref, diag_next=None, diag=None):
        for c in range(nstrip):
            if visible(c, diag_next):
                scores(j, c, dst_ref)
            consume(j - 1, c, src_ref, diag)

    def finalize(i):
        o = acc_ref[...] / l_ref[...]
        lv = lam_ref[...]
        lam = (jnp.exp(jnp.sum(lv[0:1] * lv[1:2], axis=1, keepdims=True))
               - jnp.exp(jnp.sum(lv[2:3] * lv[3:4], axis=1, keepdims=True)) + lam_init)
        a = o[:, :tq] - lam * o[:, tq:]
        ms = jnp.mean(a * a, axis=0, keepdims=True)
        y = a * lax.rsqrt(ms + EPS) * gain_ref[...] * (1.0 - lam_init)
        o_ref[pl.ds(pl.multiple_of(i * tq, tq), tq), :] = y.T.astype(o_ref.dtype)

    def pair(mm, carry):
        pipelined(2 * mm - 1, sb_ref, sa_ref)
        pipelined(2 * mm, sa_ref, sb_ref)
        return carry

    load_queries(0)
    reset_state()
    for c in range(nstrip):
        scores(0, c, sa_ref)

    def query_tile(i, carry):
        lax.fori_loop(1, i + 1, pair, 0)
        pipelined(2 * i + 1, sb_ref, sa_ref, diag_next=1, diag=0)
        load_queries(jnp.minimum(i + 1, nq - 1))
        for c in range(nstrip):
            scores(0, c, sa_ref)
            consume(2 * i + 1, c, sb_ref, diag=1)
        finalize(i)
        reset_state()
        return carry

    lax.fori_loop(0, nq, query_tile, 0)


def _attn_a(qk, vt, lamvec, gain, layer, bsz, s_len):
    n = qk.shape[0]
    tq, tk = TQ_A, TK_A
    assert tq == 2 * tk and tk % STRIP_A == 0
    nq = s_len // tq
    lam_init = 0.8 - 0.6 * math.exp(-0.3 * layer)
    return pl.pallas_call(
        functools.partial(_attn_a_kernel, tq=tq, tk=tk, strip=STRIP_A, nq=nq, lam_init=lam_init),
        grid=(bsz, A_HEADS),
        in_specs=[
            pl.BlockSpec((None, 4, HEAD_DIM), lambda b, h: (layer, 0, 0)),
            pl.BlockSpec((None, A_V, 1), lambda b, h: (layer, 0, 0)),
            pl.BlockSpec((s_len, A_V), lambda b, h: (b, h)),
            pl.BlockSpec((s_len, A_V), lambda b, h: (b, A_HEADS + h)),
            pl.BlockSpec((s_len // tk, A_V + SUM_ROWS, tk), lambda b, h: (b, h, 0)),
        ],
        out_specs=pl.BlockSpec((s_len, A_V), lambda b, h: (b, h)),
        out_shape=jax.ShapeDtypeStruct((n, A_WIDTH), BF16),
        scratch_shapes=[
            pltpu.VMEM((2 * tq, A_V), BF16),
            pltpu.VMEM((tk, 2 * tq), F32),
            pltpu.VMEM((tk, 2 * tq), F32),
            pltpu.VMEM((1, 2 * tq), F32),
            pltpu.VMEM((1, 2 * tq), F32),
            pltpu.VMEM((A_V, 2 * tq), F32),
        ],
        compiler_params=pltpu.CompilerParams(
            dimension_semantics=("arbitrary", "arbitrary"),
            vmem_limit_bytes=VMEM_LIMIT),
        name="attn_a",
    )(lamvec, gain, qk, qk, vt)


def _mix_bc_kernel(rb_ref, pw_ref, ps_ref, u_ref, up_ref, q_ref,
                   k0_ref, k1_ref, k2_ref, v0_ref, v1_ref, v2_ref,
                   o_ref, bias_ref, s_ref, *, tq):
    j = pl.program_id(1)
    nk = 3 * tq

    @pl.when((pl.program_id(0) == 0) & (j == 0))
    def _build_bias():
        rb = rb_ref[...]
        lane = lax.broadcasted_iota(jnp.int32, rb.shape, 1)
        edge = rb[:, 2 * REL_CLIP:2 * REL_CLIP + 1]
        ext = jnp.where(lane > 2 * REL_CLIP, edge, rb)
        qc = lax.broadcasted_iota(jnp.int32, (tq, nk), 0) // CHUNK
        kc = lax.broadcasted_iota(jnp.int32, (tq, nk), 1) // CHUNK
        band = (kc >= qc) & (kc <= qc + C_LEFT_CHUNKS)
        for h in range(C_HEADS):
            rows = jnp.broadcast_to(ext[h:h + 1, :], (nk, ext.shape[1]))
            rolled = pltpu.roll(rows, REL_CLIP, axis=1, stride=1, stride_axis=0)
            bias_ref[h] = jnp.where(band, rolled[:, :tq].T * LOG2E, NEG)

    u = u_ref[...]
    halo = jnp.where(j > 0, up_ref[...], 0.0)
    ext_u = jnp.concatenate([halo, u], axis=0)
    s2 = ext_u + pltpu.roll(ext_u, 1, axis=0)
    s4 = s2 + pltpu.roll(s2, 2, axis=0)
    s8 = s4 + pltpu.roll(s4, 4, axis=0)
    s16 = s8 + pltpu.roll(s8, 8, axis=0)
    grp = lax.broadcasted_iota(jnp.int32, (1, B_WIDTH), 1) // B_GROUP_DIM
    wsum = jnp.where(grp == 0, s2, jnp.where(grp == 1, s4, jnp.where(grp == 2, s8, s16)))
    wsum = wsum[POOL_HALO:, :]
    win = jnp.where(grp == 0, 2, jnp.where(grp == 1, 4, jnp.where(grp == 2, 8, 16)))
    pos = j * tq + lax.broadcasted_iota(jnp.int32, (tq, 1), 0)
    cnt = jnp.minimum(pos + 1, win).astype(F32)
    d = wsum / cnt - u
    yb = jnp.dot(d.astype(BF16), pw_ref[...], preferred_element_type=F32) * ps_ref[...]
    o_ref[:, 0:B_WIDTH] = yb.astype(o_ref.dtype)

    q = q_ref[...]
    kk = jnp.concatenate([k0_ref[...], k1_ref[...], k2_ref[...]], axis=0)
    vv = jnp.concatenate([v0_ref[...], v1_ref[...], v2_ref[...]], axis=0)
    kcol = lax.broadcasted_iota(jnp.int32, (1, nk), 1)
    first_valid = jnp.maximum(2 - j, 0) * tq
    kneg = jnp.where(kcol >= first_valid, 0.0, NEG)
    head = lax.broadcasted_iota(jnp.int32, (1, C_WIDTH), 1) // HEAD_DIM
    out = jnp.zeros((tq, C_WIDTH), F32)
    for h in range(C_HEADS):
        qh = jnp.where(head == h, q, jnp.zeros_like(q))
        s_ref[h] = _dot_nt(qh, kk)
    for h in range(C_HEADS):
        s = s_ref[h] + bias_ref[h] + kneg
        m = jnp.max(s, axis=1, keepdims=True)
        p = jnp.exp2(s - m)
        l = jnp.sum(p, axis=1, keepdims=True)
        oh = jnp.dot(p.astype(BF16), vv, preferred_element_type=F32) / l
        out = jnp.where(head == h, oh, out)
    o_ref[:, B_WIDTH:B_WIDTH + C_WIDTH] = out.astype(o_ref.dtype)


def _mix_bc(u, c, rbp, pw, ps, layer, bsz, s_len):
    n = u.shape[0]
    tq = TQ_C
    assert 2 * tq == C_LEFT_CHUNKS * CHUNK and rbp.shape[-1] == 4 * tq
    nt = s_len // tq
    hpt = tq // POOL_HALO
    cur = lambda col: (lambda b, j: (b * nt + j, col))
    prev = lambda back, col: (lambda b, j: (b * nt + jnp.maximum(j - back, 0), col))
    blk = lambda: (tq, C_WIDTH)
    return pl.pallas_call(
        functools.partial(_mix_bc_kernel, tq=tq),
        grid=(bsz, nt),
        in_specs=[
            pl.BlockSpec((None,) + rbp.shape[1:], lambda b, j: (layer, 0, 0)),
            pl.BlockSpec((None,) + pw.shape[1:], lambda b, j: (layer, 0, 0)),
            pl.BlockSpec((None, 1, B_WIDTH), lambda b, j: (layer, 0, 0)),
            pl.BlockSpec((tq, B_WIDTH), cur(0)),
            pl.BlockSpec((POOL_HALO, B_WIDTH),
                         lambda b, j: ((b * nt + j) * hpt - jnp.minimum(j, 1), 0)),
            pl.BlockSpec(blk(), cur(0)),
            pl.BlockSpec(blk(), prev(2, 1)), pl.BlockSpec(blk(), prev(1, 1)), pl.BlockSpec(blk(), cur(1)),
            pl.BlockSpec(blk(), prev(2, 2)), pl.BlockSpec(blk(), prev(1, 2)), pl.BlockSpec(blk(), cur(2)),
        ],
        out_specs=pl.BlockSpec((tq, B_WIDTH + C_WIDTH), lambda b, j: (b * nt + j, 0)),
        out_shape=jax.ShapeDtypeStruct((n, B_WIDTH + C_WIDTH), BF16),
        scratch_shapes=[pltpu.VMEM((C_HEADS, tq, 3 * tq), F32),
                        pltpu.VMEM((C_HEADS, tq, 3 * tq), F32)],
        compiler_params=pltpu.CompilerParams(
            dimension_semantics=("arbitrary", "arbitrary"), vmem_limit_bytes=VMEM_LIMIT),
        name="mix_bc",
    )(rbp, pw, ps, u, u, c, c, c, c, c, c, c)


def _post_kernel(x_ref, a_ref, bc_ref, wo_ref, g_ref, wup_ref, cw_ref, cb_ref, wd_ref, gf_ref,
                 o_ref, h_ref, acc_ref, carry_ref, a0_ref, g0_ref, a1_ref, g1_ref, hid_ref,
                 *, tiles_per_seq, final_norm):
    i = pl.program_id(0)
    tm = x_ref.shape[0]
    d_ff = wd_ref.shape[0]
    fc = a0_ref.shape[1]
    nch = d_ff // fc

    def cols(c, base=0):
        start = base + c * fc
        return pl.ds(start if isinstance(start, int) else pl.multiple_of(start, fc), fc)

    xm = (x_ref[...]
          + jnp.dot(a_ref[...], wo_ref[0:A_WIDTH, :], preferred_element_type=F32)
          + jnp.dot(bc_ref[...], wo_ref[A_WIDTH:, :], preferred_element_type=F32))
    h_ref[...] = _rms(xm, g_ref[...]).astype(BF16)
    acc_ref[...] = xm
    seq_start = (i % tiles_per_seq) == 0
    row = lax.broadcasted_iota(jnp.int32, (8, 1), 0)

    def up(c, a_ref, g_ref):
        h = h_ref[...]
        a_ref[...] = jnp.dot(h, wup_ref[:, cols(c)], preferred_element_type=F32)
        g_ref[...] = jnp.dot(h, wup_ref[:, cols(c, d_ff)], preferred_element_type=F32)

    def params(c):
        tail = jnp.where(seq_start, 0.0, carry_ref[c])
        return tail, cw_ref[:, cols(c)], cb_ref[:, cols(c)]

    def gate(c, a_ref, g_ref, prm):
        tail, cw, cb = prm
        a = a_ref[...]
        carry_ref[c] = a[tm - 8:, :]
        r1 = pltpu.roll(a, 1, axis=0)
        r2 = pltpu.roll(a, 2, axis=0)
        top1 = jnp.where(row == 0, tail[7:8, :], r1[:8, :])
        top2 = jnp.where(row == 0, tail[6:7, :], jnp.where(row == 1, tail[7:8, :], r2[:8, :]))
        a1 = jnp.concatenate([top1, r1[8:, :]], axis=0)
        a2 = jnp.concatenate([top2, r2[8:, :]], axis=0)
        conv = cb + cw[0:1, :] * a2
        conv = conv + cw[1:2, :] * a1
        conv = conv + cw[2:3, :] * a
        hid_ref[:, cols(c)] = (conv * jax.nn.sigmoid(conv) * g_ref[...]).astype(BF16)

    assert nch % 2 == 1 and nch >= 3
    up(0, a0_ref, g0_ref)
    up(1, a1_ref, g1_ref)

    def pair(pp, carry):
        c = 2 * pp
        prm0, prm1 = params(c), params(c + 1)
        gate(c, a0_ref, g0_ref, prm0)
        up(c + 2, a0_ref, g0_ref)
        gate(c + 1, a1_ref, g1_ref, prm1)
        up(c + 3, a1_ref, g1_ref)
        return carry

    npair = (nch - 3) // 2
    assert npair % 2 == 0
    lax.fori_loop(0, npair // 2, lambda qq, carry: pair(2 * qq + 1, pair(2 * qq, carry)), 0)
    gate(nch - 3, a0_ref, g0_ref, params(nch - 3))
    up(nch - 1, a0_ref, g0_ref)
    gate(nch - 2, a1_ref, g1_ref, params(nch - 2))
    gate(nch - 1, a0_ref, g0_ref, params(nch - 1))
    out = acc_ref[...] + jnp.dot(hid_ref[...], wd_ref[...], preferred_element_type=F32)
    if final_norm:
        out = _rms(out, gf_ref[...])
    o_ref[...] = out


def _post(x2, oa, obc, wo, ffn_norm, wup, cw, cb, wd, gf, layer, s_len, final_norm):
    n, d = x2.shape
    tm, fc = TM_POST, FF_CHUNK
    d_ff = wd.shape[1]
    nch = d_ff // fc
    const = pl.Buffered(1)
    l3 = lambda i: (layer, 0, 0)
    return pl.pallas_call(
        functools.partial(_post_kernel, tiles_per_seq=s_len // tm, final_norm=final_norm),
        grid=(n // tm,),
        in_specs=[
            pl.BlockSpec((tm, d), lambda i: (i, 0)),
            pl.BlockSpec((tm, A_WIDTH), lambda i: (i, 0)),
            pl.BlockSpec((tm, B_WIDTH + C_WIDTH), lambda i: (i, 0)),
            pl.BlockSpec((None,) + wo.shape[1:], l3, pipeline_mode=const),
            pl.BlockSpec((None, 1, d), l3),
            pl.BlockSpec((None,) + wup.shape[1:], l3, pipeline_mode=const),
            pl.BlockSpec((None,) + cw.shape[1:], l3),
            pl.BlockSpec((None,) + cb.shape[1:], l3),
            pl.BlockSpec((None,) + wd.shape[1:], l3, pipeline_mode=const),
            pl.BlockSpec((1, d), lambda i: (0, 0)),
        ],
        out_specs=pl.BlockSpec((tm, d), lambda i: (i, 0)),
        out_shape=jax.ShapeDtypeStruct((n, d), F32),
        scratch_shapes=[
            pltpu.VMEM((tm, d), BF16),
            pltpu.VMEM((tm, d), F32),
            pltpu.VMEM((nch, 8, fc), F32),
            pltpu.VMEM((tm, fc), F32), pltpu.VMEM((tm, fc), F32),
            pltpu.VMEM((tm, fc), F32), pltpu.VMEM((tm, fc), F32),
            pltpu.VMEM((tm, d_ff), BF16),
        ],
        compiler_params=pltpu.CompilerParams(
            dimension_semantics=("arbitrary",), vmem_limit_bytes=VMEM_LIMIT,
        ),
        name="post",
    )(x2, oa, obc, wo, ffn_norm, wup, cw, cb, wd, gf)


def kernel(x, attn_norm, w_in, lam_q1, lam_k1, lam_q2, lam_k2, diff_subln, pool_w, pool_scale,
           rel_bias, w_out, ffn_norm, w_up, conv_w, conv_b, w_down, final_norm):
    bsz, s_len, d = x.shape
    depth = w_in.shape[0]
    d_ff = w_down.shape[1]
    nch = d_ff // FF_CHUNK
    assert nch * FF_CHUNK == d_ff and s_len % TQ_A == 0 and s_len % TM_POST == 0

    w_in_b = w_in.astype(BF16)
    wvt = jnp.swapaxes(w_in[:, :, 2 * A_WIDTH:3 * A_WIDTH], 1, 2).astype(BF16)
    wo = w_out.astype(BF16)
    wup = w_up.astype(BF16)
    wd = w_down.astype(BF16)
    cw = conv_w
    cb = conv_b[:, None, :]
    lamvec = jnp.stack([lam_q1, lam_k1, lam_q2, lam_k2], axis=1)
    gain = diff_subln[:, :, None]
    eye = jnp.eye(B_GROUPS, dtype=pool_w.dtype)
    pw = jnp.einsum('lgcd,gh->lgchd', pool_w, eye).reshape(depth, B_WIDTH, B_WIDTH).astype(BF16)
    ps = pool_scale[:, None, :]
    rbp = jnp.pad(rel_bias, ((0, 0), (0, 0), (0, 4 * TQ_C - rel_bias.shape[-1])))
    an = attn_norm[:, None, :]
    fn = ffn_norm[:, None, :]
    gf = final_norm[None, :]

    x2 = x.reshape(bsz * s_len, d)
    for layer in range(depth):
        qk, vt, u, c = _in_proj(x2, an, w_in_b, wvt, layer)
        oa = _attn_a(qk, vt, lamvec, gain, layer, bsz, s_len)
        obc = _mix_bc(u, c, rbp, pw, ps, layer, bsz, s_len)
        x2 = _post(x2, oa, obc, wo, fn, wup, cw, cb, wd, gf, layer, s_len,
                   final_norm=(layer == depth - 1))
    return x2.reshape(bsz, s_len, d)
```

```python
import functools
import math

import jax
import jax.numpy as jnp
from jax import lax
from jax.experimental import pallas as pl
from jax.experimental.pallas import tpu as pltpu

F32 = jnp.float32
BF16 = jnp.bfloat16

CHUNK = 64
HEAD_DIM = 64
A_HEADS = 4
A_V = 2 * HEAD_DIM
A_WIDTH = A_HEADS * A_V
B_GROUPS = 4
B_GROUP_DIM = 64
B_WIDTH = B_GROUPS * B_GROUP_DIM
POOL_WINDOWS = (2, 4, 8, 16)
C_HEADS = 4
C_WIDTH = C_HEADS * HEAD_DIM
C_LEFT_CHUNKS = 8
REL_CLIP = 256
EPS = 1e-5
NEG = -1e30
LOG2E = math.log2(math.e)
SUM_ROWS = 16

TM_PROJ = 512
TQ_A = 1024
TK_A = 512
STRIP_A = 512
TQ_C = 256
TM_POST = 512
FF_CHUNK = 256
POOL_HALO = 16
VMEM_LIMIT = 56 * 1024 * 1024


def _rms(x, g):
    ms = jnp.mean(x * x, axis=-1, keepdims=True)
    return x * lax.rsqrt(ms + EPS) * g


def _dot_nt(a, b):
    return lax.dot_general(a, b, (((1,), (1,)), ((), ())), preferred_element_type=F32)


def _in_proj_kernel(x_ref, g_ref, w_ref, qk_ref, vt_ref, u_ref, c_ref, vct_ref, wvt_ref, wvct_ref,
                    *, tk, tc):
    v0 = 2 * A_WIDTH
    u0 = 3 * A_WIDTH
    c0 = u0 + B_WIDTH
    vc0 = c0 + 2 * C_WIDTH

    @pl.when(pl.program_id(0) == 0)
    def _transpose_value_weights():
        wvt_ref[...] = w_ref[:, v0:u0].astype(F32).T.astype(BF16)
        wvct_ref[...] = w_ref[:, vc0:].astype(F32).T.astype(BF16)

    h = _rms(x_ref[...], g_ref[...]).astype(BF16)
    qk = jnp.dot(h, w_ref[:, :v0], preferred_element_type=F32)
    col = lax.broadcasted_iota(jnp.int32, (1, qk.shape[1]), 1)
    qscale = HEAD_DIM ** -0.5 * LOG2E
    qk_ref[...] = (qk * jnp.where(col < A_WIDTH, qscale, 1.0)).astype(BF16)
    vt = _dot_nt(wvt_ref[...], h).astype(BF16)
    rows = A_V + SUM_ROWS
    for t in range(vt_ref.shape[0]):
        for hd in range(A_HEADS):
            vt_ref[t, hd * rows:hd * rows + A_V, :] = vt[hd * A_V:(hd + 1) * A_V, t * tk:(t + 1) * tk]
            vt_ref[t, hd * rows + A_V:(hd + 1) * rows, :] = jnp.ones((SUM_ROWS, tk), BF16)
    u_ref[...] = jnp.dot(h, w_ref[:, u0:c0], preferred_element_type=F32)
    c = jnp.dot(h, w_ref[:, c0:vc0], preferred_element_type=F32)
    colc = lax.broadcasted_iota(jnp.int32, (1, c.shape[1]), 1)
    c_ref[...] = (c * jnp.where(colc < C_WIDTH, qscale, 1.0)).astype(BF16)
    vct = _dot_nt(wvct_ref[...], h).astype(BF16)
    rows = HEAD_DIM + SUM_ROWS
    for t in range(vct_ref.shape[0]):
        for hd in range(C_HEADS):
            vct_ref[t, hd * rows:hd * rows + HEAD_DIM, :] = (
                vct[hd * HEAD_DIM:(hd + 1) * HEAD_DIM, t * tc:(t + 1) * tc])
            vct_ref[t, hd * rows + HEAD_DIM:(hd + 1) * rows, :] = jnp.ones((SUM_ROWS, tc), BF16)


def _in_proj(x2, attn_norm, w_in_b, layer):
    n, d = x2.shape
    tm, tk, tc = TM_PROJ, TK_A, TQ_C
    lsel = lambda *_: (layer, 0, 0)
    arows = A_HEADS * (A_V + SUM_ROWS)
    crows = C_HEADS * (HEAD_DIM + SUM_ROWS)
    return pl.pallas_call(
        functools.partial(_in_proj_kernel, tk=tk, tc=tc),
        grid=(n // tm,),
        in_specs=[
            pl.BlockSpec((tm, d), lambda i: (i, 0)),
            pl.BlockSpec((None, 1, d), lsel),
            pl.BlockSpec((None,) + w_in_b.shape[1:], lsel),
        ],
        out_specs=[
            pl.BlockSpec((tm, 2 * A_WIDTH), lambda i: (i, 0)),
            pl.BlockSpec((tm // tk, arows, tk), lambda i: (i, 0, 0)),
            pl.BlockSpec((tm, B_WIDTH), lambda i: (i, 0)),
            pl.BlockSpec((tm, 2 * C_WIDTH), lambda i: (i, 0)),
            pl.BlockSpec((tm // tc, crows, tc), lambda i: (i, 0, 0)),
        ],
        out_shape=[
            jax.ShapeDtypeStruct((n, 2 * A_WIDTH), BF16),
            jax.ShapeDtypeStruct((n // tk, arows, tk), BF16),
            jax.ShapeDtypeStruct((n, B_WIDTH), F32),
            jax.ShapeDtypeStruct((n, 2 * C_WIDTH), BF16),
            jax.ShapeDtypeStruct((n // tc, crows, tc), BF16),
        ],
        scratch_shapes=[
            pltpu.VMEM((A_WIDTH, d), BF16),
            pltpu.VMEM((C_WIDTH, d), BF16),
        ],
        compiler_params=pltpu.CompilerParams(
            dimension_semantics=("arbitrary",), vmem_limit_bytes=VMEM_LIMIT),
        name="in_proj",
    )(x2, attn_norm, w_in_b)


def _attn_a_kernel(lam_ref, gain_ref, q_ref, k_ref, vt_ref, o_ref,
                   qcat_ref, sa_ref, sb_ref, m_ref, l_ref, acc_ref,
                   *, tq, tk, strip, nq, lam_init):
    nstrip = 2 * tq // strip

    def load_queries(i):
        q = q_ref[pl.ds(pl.multiple_of(i * tq, tq), tq), :]
        lane = lax.broadcasted_iota(jnp.int32, q.shape, 1)
        zero = jnp.zeros_like(q)
        qcat_ref[0:tq, :] = jnp.where(lane < HEAD_DIM, q, zero)
        qcat_ref[tq:2 * tq, :] = jnp.where(lane >= HEAD_DIM, q, zero)

    def reset_state():
        m_ref[...] = jnp.full(m_ref.shape, NEG, F32)
        l_ref[...] = jnp.zeros(l_ref.shape, F32)
        acc_ref[...] = jnp.zeros(acc_ref.shape, F32)

    def scores(j, c, dst_ref):
        cs = slice(c * strip, (c + 1) * strip)
        k_t = k_ref[pl.ds(pl.multiple_of(j * tk, tk), tk), :]
        dst_ref[:, cs] = _dot_nt(k_t, qcat_ref[cs, :])

    def visible(c, diag):
        qblock = ((c * strip) % tq) // tk
        return 2 if diag is None or qblock > diag else int(qblock == diag)

    def consume(t, c, src_ref, diag=None):
        if visible(c, diag) == 0:
            return
        cs = slice(c * strip, (c + 1) * strip)
        s = src_ref[:, cs]
        if visible(c, diag) == 1:
            kc = lax.broadcasted_iota(jnp.int32, s.shape, 0) // CHUNK
            qc = ((c * strip) % tk + lax.broadcasted_iota(jnp.int32, s.shape, 1)) // CHUNK
            s = jnp.where(kc <= qc, s, NEG)
        m_old = m_ref[:, cs]
        m_new = jnp.maximum(m_old, jnp.max(s, axis=0, keepdims=True))
        alpha = jnp.exp2(m_old - m_new)
        p = jnp.exp2(s - m_new).astype(BF16)
        pv = jnp.dot(vt_ref[t], p, preferred_element_type=F32)
        l_ref[:, cs] = alpha * l_ref[:, cs] + pv[A_V:A_V + 1, :]
        acc_ref[:, cs] = alpha * acc_ref[:, cs] + pv[:A_V, :]
        m_ref[:, cs] = m_new

    def pipelined(j, dst_ref, src_ref, diag_next=None, diag=None):
        for c in range(nstrip):
            if visible(c, diag_next):
                scores(j, c, dst_ref)
            consume(j - 1, c, src_ref, diag)

    def finalize(i):
        o = acc_ref[...] / l_ref[...]
        lv = lam_ref[...]
        lam = (jnp.exp(jnp.sum(lv[0:1] * lv[1:2], axis=1, keepdims=True))
               - jnp.exp(jnp.sum(lv[2:3] * lv[3:4], axis=1, keepdims=True)) + lam_init)
        a = o[:, :tq] - lam * o[:, tq:]
        ms = jnp.mean(a * a, axis=0, keepdims=True)
        y = a * lax.rsqrt(ms + EPS) * gain_ref[...] * (1.0 - lam_init)
        o_ref[pl.ds(pl.multiple_of(i * tq, tq), tq), :] = y.T.astype(o_ref.dtype)

    def pair(mm, carry):
        pipelined(2 * mm - 1, sb_ref, sa_ref)
        pipelined(2 * mm, sa_ref, sb_ref)
        return carry

    load_queries(0)
    reset_state()
    for c in range(nstrip):
        scores(0, c, sa_ref)

    def query_tile(i, carry):
        lax.fori_loop(1, i + 1, pair, 0)
        pipelined(2 * i + 1, sb_ref, sa_ref, diag_next=1, diag=0)
        load_queries(jnp.minimum(i + 1, nq - 1))
        for c in range(nstrip):
            scores(0, c, sa_ref)
            consume(2 * i + 1, c, sb_ref, diag=1)
        finalize(i)
        reset_state()
        return carry

    lax.fori_loop(0, nq, query_tile, 0)


def _attn_a(qk, vt, lamvec, gain, layer, bsz, s_len):
    n = qk.shape[0]
    tq, tk = TQ_A, TK_A
    assert tq == 2 * tk and tk % STRIP_A == 0
    nq = s_len // tq
    lam_init = 0.8 - 0.6 * math.exp(-0.3 * layer)
    return pl.pallas_call(
        functools.partial(_attn_a_kernel, tq=tq, tk=tk, strip=STRIP_A, nq=nq, lam_init=lam_init),
        grid=(bsz, A_HEADS),
        in_specs=[
            pl.BlockSpec((None, 4, HEAD_DIM), lambda b, h: (layer, 0, 0)),
            pl.BlockSpec((None, A_V, 1), lambda b, h: (layer, 0, 0)),
            pl.BlockSpec((s_len, A_V), lambda b, h: (b, h)),
            pl.BlockSpec((s_len, A_V), lambda b, h: (b, A_HEADS + h)),
            pl.BlockSpec((s_len // tk, A_V + SUM_ROWS, tk), lambda b, h: (b, h, 0)),
        ],
        out_specs=pl.BlockSpec((s_len, A_V), lambda b, h: (b, h)),
        out_shape=jax.ShapeDtypeStruct((n, A_WIDTH), BF16),
        scratch_shapes=[
            pltpu.VMEM((2 * tq, A_V), BF16),
            pltpu.VMEM((tk, 2 * tq), F32),
            pltpu.VMEM((tk, 2 * tq), F32),
            pltpu.VMEM((1, 2 * tq), F32),
            pltpu.VMEM((1, 2 * tq), F32),
            pltpu.VMEM((A_V, 2 * tq), F32),
        ],
        compiler_params=pltpu.CompilerParams(
            dimension_semantics=("arbitrary", "arbitrary"),
            vmem_limit_bytes=VMEM_LIMIT),
        name="attn_a",
    )(lamvec, gain, qk, qk, vt)


def _mix_bc_kernel(rb_ref, pw_ref, ps_ref, u_ref, up_ref, q_ref,
                   k0_ref, k1_ref, k2_ref, v0_ref, v1_ref, v2_ref,
                   o_ref, bias_ref, s_ref, *, tq):
    j = pl.program_id(1)
    nk = 3 * tq

    @pl.when((pl.program_id(0) == 0) & (j == 0))
    def _build_bias():
        rb = rb_ref[...]
        lane = lax.broadcasted_iota(jnp.int32, rb.shape, 1)
        edge = rb[:, 2 * REL_CLIP:2 * REL_CLIP + 1]
        ext = jnp.where(lane > 2 * REL_CLIP, edge, rb)
        kc = lax.broadcasted_iota(jnp.int32, (nk, tq), 0) // CHUNK
        qc = lax.broadcasted_iota(jnp.int32, (nk, tq), 1) // CHUNK
        band = (kc >= qc) & (kc <= qc + C_LEFT_CHUNKS)
        for h in range(C_HEADS):
            rows = jnp.broadcast_to(ext[h:h + 1, :], (nk, ext.shape[1]))
            rolled = pltpu.roll(rows, REL_CLIP, axis=1, stride=1, stride_axis=0)
            bias_ref[h] = jnp.where(band, rolled[:, :tq] * LOG2E, NEG)

    u = u_ref[...]
    halo = jnp.where(j > 0, up_ref[...], 0.0)
    ext_u = jnp.concatenate([halo, u], axis=0)
    s2 = ext_u + pltpu.roll(ext_u, 1, axis=0)
    s4 = s2 + pltpu.roll(s2, 2, axis=0)
    s8 = s4 + pltpu.roll(s4, 4, axis=0)
    s16 = s8 + pltpu.roll(s8, 8, axis=0)
    grp = lax.broadcasted_iota(jnp.int32, (1, B_WIDTH), 1) // B_GROUP_DIM
    wsum = jnp.where(grp == 0, s2, jnp.where(grp == 1, s4, jnp.where(grp == 2, s8, s16)))
    wsum = wsum[POOL_HALO:, :]
    win = jnp.where(grp == 0, 2, jnp.where(grp == 1, 4, jnp.where(grp == 2, 8, 16)))
    pos = j * tq + lax.broadcasted_iota(jnp.int32, (tq, 1), 0)
    cnt = jnp.minimum(pos + 1, win).astype(F32)
    d = wsum / cnt - u
    yb = jnp.dot(d.astype(BF16), pw_ref[...], preferred_element_type=F32) * ps_ref[...]
    o_ref[:, 0:B_WIDTH] = yb.astype(o_ref.dtype)

    q = q_ref[...]
    kk = jnp.concatenate([k0_ref[...], k1_ref[...], k2_ref[...]], axis=0)
    vts = (v0_ref, v1_ref, v2_ref)
    head = lax.broadcasted_iota(jnp.int32, (1, C_WIDTH), 1) // HEAD_DIM
    for h in range(C_HEADS):
        qh = jnp.where(head == h, q, jnp.zeros_like(q))
        s_ref[h] = _dot_nt(kk, qh)
    rows = HEAD_DIM + SUM_ROWS
    krow = lax.broadcasted_iota(jnp.int32, (nk, tq), 0)
    kneg = jnp.where(krow >= (2 - j) * tq, 0.0, NEG)
    outs = []
    for h in range(C_HEADS):
        s = s_ref[h] + bias_ref[h] + kneg
        m = jnp.max(s, axis=0, keepdims=True)
        p = jnp.exp2(s - m).astype(BF16)
        pv = jnp.dot(vts[0][h * rows:(h + 1) * rows, :], p[0:tq, :], preferred_element_type=F32)
        for t in (1, 2):
            pv = pv + jnp.dot(vts[t][h * rows:(h + 1) * rows, :], p[t * tq:(t + 1) * tq, :],
                              preferred_element_type=F32)
        outs.append(pv[:HEAD_DIM, :] / pv[HEAD_DIM:HEAD_DIM + 1, :])
    out = jnp.concatenate(outs, axis=0).T
    o_ref[:, B_WIDTH:B_WIDTH + C_WIDTH] = out.astype(o_ref.dtype)


def _mix_bc(u, c, vct, rbp, pw, ps, layer, bsz, s_len):
    n = u.shape[0]
    tq = TQ_C
    assert 2 * tq == C_LEFT_CHUNKS * CHUNK and rbp.shape[-1] == 4 * tq
    nt = s_len // tq
    hpt = tq // POOL_HALO
    crows = vct.shape[1]
    cur = lambda col: (lambda b, j: (b * nt + j, col))
    prev = lambda back, col: (lambda b, j: (b * nt + jnp.maximum(j - back, 0), col))
    blk = lambda: (tq, C_WIDTH)
    vblk = lambda back: pl.BlockSpec((None, crows, tq),
                                     lambda b, j: (b * nt + jnp.maximum(j - back, 0), 0, 0))
    return pl.pallas_call(
        functools.partial(_mix_bc_kernel, tq=tq),
        grid=(bsz, nt),
        in_specs=[
            pl.BlockSpec((None,) + rbp.shape[1:], lambda b, j: (layer, 0, 0)),
            pl.BlockSpec((None,) + pw.shape[1:], lambda b, j: (layer, 0, 0)),
            pl.BlockSpec((None, 1, B_WIDTH), lambda b, j: (layer, 0, 0)),
            pl.BlockSpec((tq, B_WIDTH), cur(0)),
            pl.BlockSpec((POOL_HALO, B_WIDTH),
                         lambda b, j: ((b * nt + j) * hpt - jnp.minimum(j, 1), 0)),
            pl.BlockSpec(blk(), cur(0)),
            pl.BlockSpec(blk(), prev(2, 1)), pl.BlockSpec(blk(), prev(1, 1)), pl.BlockSpec(blk(), cur(1)),
            vblk(2), vblk(1), vblk(0),
        ],
        out_specs=pl.BlockSpec((tq, B_WIDTH + C_WIDTH), lambda b, j: (b * nt + j, 0)),
        out_shape=jax.ShapeDtypeStruct((n, B_WIDTH + C_WIDTH), BF16),
        scratch_shapes=[pltpu.VMEM((C_HEADS, 3 * tq, tq), F32),
                        pltpu.VMEM((C_HEADS, 3 * tq, tq), F32)],
        compiler_params=pltpu.CompilerParams(
            dimension_semantics=("arbitrary", "arbitrary"), vmem_limit_bytes=VMEM_LIMIT),
        name="mix_bc",
    )(rbp, pw, ps, u, u, c, c, c, c, vct, vct, vct)


def _post_kernel(x_ref, a_ref, bc_ref, wo_ref, g_ref, wup_ref, cw_ref, cb_ref, wd_ref, gf_ref,
                 o_ref, h_ref, acc_ref, carry_ref, a0_ref, g0_ref, a1_ref, g1_ref, hid_ref,
                 *, tiles_per_seq, final_norm):
    i = pl.program_id(0)
    tm = x_ref.shape[0]
    d_ff = wd_ref.shape[0]
    fc = a0_ref.shape[1]
    nch = d_ff // fc

    def cols(c, base=0):
        start = base + c * fc
        return pl.ds(start if isinstance(start, int) else pl.multiple_of(start, fc), fc)

    xm = (x_ref[...]
          + jnp.dot(a_ref[...], wo_ref[0:A_WIDTH, :], preferred_element_type=F32)
          + jnp.dot(bc_ref[...], wo_ref[A_WIDTH:, :], preferred_element_type=F32))
    h_ref[...] = _rms(xm, g_ref[...]).astype(BF16)
    acc_ref[...] = xm
    seq_start = (i % tiles_per_seq) == 0
    row = lax.broadcasted_iota(jnp.int32, (8, 1), 0)

    def up(c, a_ref, g_ref):
        h = h_ref[...]
        a_ref[...] = jnp.dot(h, wup_ref[:, cols(c)], preferred_element_type=F32)
        g_ref[...] = jnp.dot(h, wup_ref[:, cols(c, d_ff)], preferred_element_type=F32)

    def params(c):
        tail = jnp.where(seq_start, 0.0, carry_ref[c])
        return tail, cw_ref[:, cols(c)], cb_ref[:, cols(c)]

    def gate(c, a_ref, g_ref, prm):
        tail, cw, cb = prm
        a = a_ref[...]
        carry_ref[c] = a[tm - 8:, :]
        r1 = pltpu.roll(a, 1, axis=0)
        r2 = pltpu.roll(a, 2, axis=0)
        top1 = jnp.where(row == 0, tail[7:8, :], r1[:8, :])
        top2 = jnp.where(row == 0, tail[6:7, :], jnp.where(row == 1, tail[7:8, :], r2[:8, :]))
        a1 = jnp.concatenate([top1, r1[8:, :]], axis=0)
        a2 = jnp.concatenate([top2, r2[8:, :]], axis=0)
        conv = cb + cw[0:1, :] * a2
        conv = conv + cw[1:2, :] * a1
        conv = conv + cw[2:3, :] * a
        hid_ref[:, cols(c)] = (conv * jax.nn.sigmoid(conv) * g_ref[...]).astype(BF16)

    assert nch % 2 == 1 and nch >= 3
    up(0, a0_ref, g0_ref)
    up(1, a1_ref, g1_ref)

    def pair(pp, carry):
        c = 2 * pp
        prm0, prm1 = params(c), params(c + 1)
        gate(c, a0_ref, g0_ref, prm0)
        up(c + 2, a0_ref, g0_ref)
        gate(c + 1, a1_ref, g1_ref, prm1)
        up(c + 3, a1_ref, g1_ref)
        return carry

    npair = (nch - 3) // 2
    assert npair % 2 == 0
    lax.fori_loop(0, npair // 2, lambda qq, carry: pair(2 * qq + 1, pair(2 * qq, carry)), 0)
    gate(nch - 3, a0_ref, g0_ref, params(nch - 3))
    up(nch - 1, a0_ref, g0_ref)
    gate(nch - 2, a1_ref, g1_ref, params(nch - 2))
    gate(nch - 1, a0_ref, g0_ref, params(nch - 1))
    out = acc_ref[...] + jnp.dot(hid_ref[...], wd_ref[...], preferred_element_type=F32)
    if final_norm:
        out = _rms(out, gf_ref[...])
    o_ref[...] = out


def _post(x2, oa, obc, wo, ffn_norm, wup, cw, cb, wd, gf, layer, s_len, final_norm):
    n, d = x2.shape
    tm, fc = TM_POST, FF_CHUNK
    d_ff = wd.shape[1]
    nch = d_ff // fc
    const = pl.Buffered(1)
    l3 = lambda i: (layer, 0, 0)
    return pl.pallas_call(
        functools.partial(_post_kernel, tiles_per_seq=s_len // tm, final_norm=final_norm),
        grid=(n // tm,),
        in_specs=[
            pl.BlockSpec((tm, d), lambda i: (i, 0)),
            pl.BlockSpec((tm, A_WIDTH), lambda i: (i, 0)),
            pl.BlockSpec((tm, B_WIDTH + C_WIDTH), lambda i: (i, 0)),
            pl.BlockSpec((None,) + wo.shape[1:], l3, pipeline_mode=const),
            pl.BlockSpec((None, 1, d), l3),
            pl.BlockSpec((None,) + wup.shape[1:], l3, pipeline_mode=const),
            pl.BlockSpec((None,) + cw.shape[1:], l3),
            pl.BlockSpec((None,) + cb.shape[1:], l3),
            pl.BlockSpec((None,) + wd.shape[1:], l3, pipeline_mode=const),
            pl.BlockSpec((1, d), lambda i: (0, 0)),
        ],
        out_specs=pl.BlockSpec((tm, d), lambda i: (i, 0)),
        out_shape=jax.ShapeDtypeStruct((n, d), F32),
        scratch_shapes=[
            pltpu.VMEM((tm, d), BF16),
            pltpu.VMEM((tm, d), F32),
            pltpu.VMEM((nch, 8, fc), F32),
            pltpu.VMEM((tm, fc), F32), pltpu.VMEM((tm, fc), F32),
            pltpu.VMEM((tm, fc), F32), pltpu.VMEM((tm, fc), F32),
            pltpu.VMEM((tm, d_ff), BF16),
        ],
        compiler_params=pltpu.CompilerParams(
            dimension_semantics=("arbitrary",), vmem_limit_bytes=VMEM_LIMIT),
        name="post",
    )(x2, oa, obc, wo, ffn_norm, wup, cw, cb, wd, gf)


def kernel(x, attn_norm, w_in, lam_q1, lam_k1, lam_q2, lam_k2, diff_subln, pool_w, pool_scale,
           rel_bias, w_out, ffn_norm, w_up, conv_w, conv_b, w_down, final_norm):
    bsz, s_len, d = x.shape
    depth = w_in.shape[0]
    d_ff = w_down.shape[1]
    nch = d_ff // FF_CHUNK
    assert nch * FF_CHUNK == d_ff and s_len % TQ_A == 0 and s_len % TM_POST == 0

    w_in_b = w_in.astype(BF16)
    wo = w_out.astype(BF16)
    wup = w_up.astype(BF16)
    wd = w_down.astype(BF16)
    cw = conv_w
    cb = conv_b[:, None, :]
    lamvec = jnp.stack([lam_q1, lam_k1, lam_q2, lam_k2], axis=1)
    gain = diff_subln[:, :, None]
    eye = jnp.eye(B_GROUPS, dtype=pool_w.dtype)
    pw = jnp.einsum('lgcd,gh->lgchd', pool_w, eye).reshape(depth, B_WIDTH, B_WIDTH).astype(BF16)
    ps = pool_scale[:, None, :]
    rbp = jnp.pad(rel_bias, ((0, 0), (0, 0), (0, 4 * TQ_C - rel_bias.shape[-1])))
    an = attn_norm[:, None, :]
    fn = ffn_norm[:, None, :]
    gf = final_norm[None, :]

    x2 = x.reshape(bsz * s_len, d)
    for layer in range(depth):
        qk, vt, u, c, vct = _in_proj(x2, an, w_in_b, layer)
        oa = _attn_a(qk, vt, lamvec, gain, layer, bsz, s_len)
        obc = _mix_bc(u, c, vct, rbp, pw, ps, layer, bsz, s_len)
        x2 = _post(x2, oa, obc, wo, fn, wup, cw, cb, wd, gf, layer, s_len,
                   final_norm=(layer == depth - 1))
    return x2.reshape(bsz, s_len, d)
```

```python
import functools
import math

import jax
import jax.numpy as jnp
from jax import lax
from jax.experimental import pallas as pl
from jax.experimental.pallas import tpu as pltpu

F32 = jnp.float32
BF16 = jnp.bfloat16

CHUNK = 64
HEAD_DIM = 64
A_HEADS = 4
A_V = 2 * HEAD_DIM
A_WIDTH = A_HEADS * A_V
B_GROUPS = 4
B_GROUP_DIM = 64
B_WIDTH = B_GROUPS * B_GROUP_DIM
POOL_WINDOWS = (2, 4, 8, 16)
C_HEADS = 4
C_WIDTH = C_HEADS * HEAD_DIM
C_LEFT_CHUNKS = 8
REL_CLIP = 256
EPS = 1e-5
NEG = -1e30
LOG2E = math.log2(math.e)
SUM_ROWS = 16

TM_PROJ = 512
TQ_A = 2048
TK_A = 512
STRIP_A = 512
TQ_C = 256
TM_POST = 512
FF_CHUNK = 256
POOL_HALO = 16
VMEM_LIMIT = 56 * 1024 * 1024


def _rms(x, g):
    ms = jnp.mean(x * x, axis=-1, keepdims=True)
    return x * lax.rsqrt(ms + EPS) * g


def _dot_nt(a, b):
    return lax.dot_general(a, b, (((1,), (1,)), ((), ())), preferred_element_type=F32)


def _in_proj_kernel(x_ref, g_ref, w_ref, qk_ref, vt_ref, u_ref, c_ref, vct_ref, wvt_ref, wvct_ref,
                    *, tk, tc):
    v0 = 2 * A_WIDTH
    u0 = 3 * A_WIDTH
    c0 = u0 + B_WIDTH
    vc0 = c0 + 2 * C_WIDTH

    @pl.when(pl.program_id(0) == 0)
    def _transpose_value_weights():
        wvt_ref[...] = w_ref[:, v0:u0].astype(F32).T.astype(BF16)
        wvct_ref[...] = w_ref[:, vc0:].astype(F32).T.astype(BF16)

    h = _rms(x_ref[...], g_ref[...]).astype(BF16)
    qk = jnp.dot(h, w_ref[:, :v0], preferred_element_type=F32)
    col = lax.broadcasted_iota(jnp.int32, (1, qk.shape[1]), 1)
    qscale = HEAD_DIM ** -0.5 * LOG2E
    qk_ref[...] = (qk * jnp.where(col < A_WIDTH, qscale, 1.0)).astype(BF16)
    vt = _dot_nt(wvt_ref[...], h).astype(BF16)
    rows = A_V + SUM_ROWS
    for t in range(vt_ref.shape[0]):
        for hd in range(A_HEADS):
            vt_ref[t, hd * rows:hd * rows + A_V, :] = vt[hd * A_V:(hd + 1) * A_V, t * tk:(t + 1) * tk]
            vt_ref[t, hd * rows + A_V:(hd + 1) * rows, :] = jnp.ones((SUM_ROWS, tk), BF16)
    u_ref[...] = jnp.dot(h, w_ref[:, u0:c0], preferred_element_type=F32)
    c = jnp.dot(h, w_ref[:, c0:vc0], preferred_element_type=F32)
    colc = lax.broadcasted_iota(jnp.int32, (1, c.shape[1]), 1)
    c_ref[...] = (c * jnp.where(colc < C_WIDTH, qscale, 1.0)).astype(BF16)
    vct = _dot_nt(wvct_ref[...], h).astype(BF16)
    rows = HEAD_DIM + SUM_ROWS
    for t in range(vct_ref.shape[0]):
        for hd in range(C_HEADS):
            vct_ref[t, hd * rows:hd * rows + HEAD_DIM, :] = (
                vct[hd * HEAD_DIM:(hd + 1) * HEAD_DIM, t * tc:(t + 1) * tc])
            vct_ref[t, hd * rows + HEAD_DIM:(hd + 1) * rows, :] = jnp.ones((SUM_ROWS, tc), BF16)


def _in_proj(x2, attn_norm, w_in_b, layer):
    n, d = x2.shape
    tm, tk, tc = TM_PROJ, TK_A, TQ_C
    lsel = lambda *_: (layer, 0, 0)
    arows = A_HEADS * (A_V + SUM_ROWS)
    crows = C_HEADS * (HEAD_DIM + SUM_ROWS)
    return pl.pallas_call(
        functools.partial(_in_proj_kernel, tk=tk, tc=tc),
        grid=(n // tm,),
        in_specs=[
            pl.BlockSpec((tm, d), lambda i: (i, 0)),
            pl.BlockSpec((None, 1, d), lsel),
            pl.BlockSpec((None,) + w_in_b.shape[1:], lsel),
        ],
        out_specs=[
            pl.BlockSpec((tm, 2 * A_WIDTH), lambda i: (i, 0)),
            pl.BlockSpec((tm // tk, arows, tk), lambda i: (i, 0, 0)),
            pl.BlockSpec((tm, B_WIDTH), lambda i: (i, 0)),
            pl.BlockSpec((tm, 2 * C_WIDTH), lambda i: (i, 0)),
            pl.BlockSpec((tm // tc, crows, tc), lambda i: (i, 0, 0)),
        ],
        out_shape=[
            jax.ShapeDtypeStruct((n, 2 * A_WIDTH), BF16),
            jax.ShapeDtypeStruct((n // tk, arows, tk), BF16),
            jax.ShapeDtypeStruct((n, B_WIDTH), F32),
            jax.ShapeDtypeStruct((n, 2 * C_WIDTH), BF16),
            jax.ShapeDtypeStruct((n // tc, crows, tc), BF16),
        ],
        scratch_shapes=[
            pltpu.VMEM((A_WIDTH, d), BF16),
            pltpu.VMEM((C_WIDTH, d), BF16),
        ],
        compiler_params=pltpu.CompilerParams(
            dimension_semantics=("arbitrary",), vmem_limit_bytes=VMEM_LIMIT),
        name="in_proj",
    )(x2, attn_norm, w_in_b)


def _attn_a_kernel(lam_ref, gain_ref, q_ref, k_ref, vt_ref, o_ref,
                   qcat_ref, sa_ref, sb_ref, m_ref, l_ref, acc_ref,
                   *, tq, tk, strip, nq, lam_init):
    nstrip = 2 * tq // strip

    def load_queries(i):
        q = q_ref[pl.ds(pl.multiple_of(i * tq, tq), tq), :]
        lane = lax.broadcasted_iota(jnp.int32, q.shape, 1)
        zero = jnp.zeros_like(q)
        qcat_ref[0:tq, :] = jnp.where(lane < HEAD_DIM, q, zero)
        qcat_ref[tq:2 * tq, :] = jnp.where(lane >= HEAD_DIM, q, zero)

    def reset_state():
        m_ref[...] = jnp.full(m_ref.shape, NEG, F32)
        l_ref[...] = jnp.zeros(l_ref.shape, F32)
        acc_ref[...] = jnp.zeros(acc_ref.shape, F32)

    def scores(j, c, dst_ref):
        cs = slice(c * strip, (c + 1) * strip)
        k_t = k_ref[pl.ds(pl.multiple_of(j * tk, tk), tk), :]
        dst_ref[:, cs] = _dot_nt(k_t, qcat_ref[cs, :])

    def visible(c, diag):
        qblock = ((c * strip) % tq) // tk
        return 2 if diag is None or qblock > diag else int(qblock == diag)

    def consume(t, c, src_ref, diag=None):
        if visible(c, diag) == 0:
            return
        cs = slice(c * strip, (c + 1) * strip)
        s = src_ref[:, cs]
        if visible(c, diag) == 1:
            kc = lax.broadcasted_iota(jnp.int32, s.shape, 0) // CHUNK
            qc = ((c * strip) % tk + lax.broadcasted_iota(jnp.int32, s.shape, 1)) // CHUNK
            s = jnp.where(kc <= qc, s, NEG)
        m_old = m_ref[:, cs]
        m_new = jnp.maximum(m_old, jnp.max(s, axis=0, keepdims=True))
        alpha = jnp.exp2(m_old - m_new)
        p = jnp.exp2(s - m_new).astype(BF16)
        pv = jnp.dot(vt_ref[t], p, preferred_element_type=F32)
        l_ref[:, cs] = alpha * l_ref[:, cs] + pv[A_V:A_V + 1, :]
        acc_ref[:, cs] = alpha * acc_ref[:, cs] + pv[:A_V, :]
        m_ref[:, cs] = m_new

    def pipelined(j, dst_ref, src_ref, diag_next=None, diag=None):
        for c in range(nstrip):
            if visible(c, diag_next):
                scores(j, c, dst_ref)
            consume(j - 1, c, src_ref, diag)

    def finalize(i):
        o = acc_ref[...] / l_ref[...]
        lv = lam_ref[...]
        lam = (jnp.exp(jnp.sum(lv[0:1] * lv[1:2], axis=1, keepdims=True))
               - jnp.exp(jnp.sum(lv[2:3] * lv[3:4], axis=1, keepdims=True)) + lam_init)
        a = o[:, :tq] - lam * o[:, tq:]
        ms = jnp.mean(a * a, axis=0, keepdims=True)
        y = a * lax.rsqrt(ms + EPS) * gain_ref[...] * (1.0 - lam_init)
        o_ref[pl.ds(pl.multiple_of(i * tq, tq), tq), :] = y.T.astype(o_ref.dtype)

    def pair(mm, carry):
        pipelined(2 * mm - 1, sb_ref, sa_ref)
        pipelined(2 * mm, sa_ref, sb_ref)
        return carry

    ratio = tq // tk
    assert ratio % 2 == 0
    load_queries(0)
    reset_state()
    for c in range(nstrip):
        scores(0, c, sa_ref)

    def query_tile(i, carry):
        first = ratio * i
        lax.fori_loop(1, first // 2 + 1, pair, 0)
        bufs = (sa_ref, sb_ref)
        for r in range(ratio - 1):
            pipelined(first + r + 1, bufs[(r + 1) % 2], bufs[r % 2], diag_next=r + 1, diag=r)
        load_queries(jnp.minimum(i + 1, nq - 1))
        for c in range(nstrip):
            scores(0, c, sa_ref)
            consume(first + ratio - 1, c, sb_ref, diag=ratio - 1)
        finalize(i)
        reset_state()
        return carry

    lax.fori_loop(0, nq, query_tile, 0)


def _attn_a(qk, vt, lamvec, gain, layer, bsz, s_len):
    n = qk.shape[0]
    tq, tk = TQ_A, TK_A
    assert tq % (2 * tk) == 0 and tk % STRIP_A == 0
    nq = s_len // tq
    lam_init = 0.8 - 0.6 * math.exp(-0.3 * layer)
    return pl.pallas_call(
        functools.partial(_attn_a_kernel, tq=tq, tk=tk, strip=STRIP_A, nq=nq, lam_init=lam_init),
        grid=(bsz, A_HEADS),
        in_specs=[
            pl.BlockSpec((None, 4, HEAD_DIM), lambda b, h: (layer, 0, 0)),
            pl.BlockSpec((None, A_V, 1), lambda b, h: (layer, 0, 0)),
            pl.BlockSpec((s_len, A_V), lambda b, h: (b, h)),
            pl.BlockSpec((s_len, A_V), lambda b, h: (b, A_HEADS + h)),
            pl.BlockSpec((s_len // tk, A_V + SUM_ROWS, tk), lambda b, h: (b, h, 0)),
        ],
        out_specs=pl.BlockSpec((s_len, A_V), lambda b, h: (b, h)),
        out_shape=jax.ShapeDtypeStruct((n, A_WIDTH), BF16),
        scratch_shapes=[
            pltpu.VMEM((2 * tq, A_V), BF16),
            pltpu.VMEM((tk, 2 * tq), F32),
            pltpu.VMEM((tk, 2 * tq), F32),
            pltpu.VMEM((1, 2 * tq), F32),
            pltpu.VMEM((1, 2 * tq), F32),
            pltpu.VMEM((A_V, 2 * tq), F32),
        ],
        compiler_params=pltpu.CompilerParams(
            dimension_semantics=("arbitrary", "arbitrary"),
            vmem_limit_bytes=VMEM_LIMIT),
        name="attn_a",
    )(lamvec, gain, qk, qk, vt)


def _mix_bc_kernel(rb_ref, pw_ref, ps_ref, u_ref, up_ref, q_ref,
                   k0_ref, k1_ref, k2_ref, v0_ref, v1_ref, v2_ref,
                   o_ref, bias_ref, s_ref, *, tq):
    j = pl.program_id(1)
    nk = 3 * tq

    @pl.when((pl.program_id(0) == 0) & (j == 0))
    def _build_bias():
        rb = rb_ref[...]
        lane = lax.broadcasted_iota(jnp.int32, rb.shape, 1)
        edge = rb[:, 2 * REL_CLIP:2 * REL_CLIP + 1]
        ext = jnp.where(lane > 2 * REL_CLIP, edge, rb)
        kc = lax.broadcasted_iota(jnp.int32, (nk, tq), 0) // CHUNK
        qc = lax.broadcasted_iota(jnp.int32, (nk, tq), 1) // CHUNK
        band = (kc >= qc) & (kc <= qc + C_LEFT_CHUNKS)
        for h in range(C_HEADS):
            rows = jnp.broadcast_to(ext[h:h + 1, :], (nk, ext.shape[1]))
            rolled = pltpu.roll(rows, REL_CLIP, axis=1, stride=1, stride_axis=0)
            bias_ref[h] = jnp.where(band, rolled[:, :tq] * LOG2E, NEG)

    u = u_ref[...]
    halo = jnp.where(j > 0, up_ref[...], 0.0)
    ext_u = jnp.concatenate([halo, u], axis=0)
    s2 = ext_u + pltpu.roll(ext_u, 1, axis=0)
    s4 = s2 + pltpu.roll(s2, 2, axis=0)
    s8 = s4 + pltpu.roll(s4, 4, axis=0)
    s16 = s8 + pltpu.roll(s8, 8, axis=0)
    grp = lax.broadcasted_iota(jnp.int32, (1, B_WIDTH), 1) // B_GROUP_DIM
    wsum = jnp.where(grp == 0, s2, jnp.where(grp == 1, s4, jnp.where(grp == 2, s8, s16)))
    wsum = wsum[POOL_HALO:, :]
    win = jnp.where(grp == 0, 2, jnp.where(grp == 1, 4, jnp.where(grp == 2, 8, 16)))
    pos = j * tq + lax.broadcasted_iota(jnp.int32, (tq, 1), 0)
    cnt = jnp.minimum(pos + 1, win).astype(F32)
    d = wsum / cnt - u
    yb = jnp.dot(d.astype(BF16), pw_ref[...], preferred_element_type=F32) * ps_ref[...]
    o_ref[:, 0:B_WIDTH] = yb.astype(o_ref.dtype)

    q = q_ref[...]
    kk = jnp.concatenate([k0_ref[...], k1_ref[...], k2_ref[...]], axis=0)
    vts = (v0_ref, v1_ref, v2_ref)
    head = lax.broadcasted_iota(jnp.int32, (1, C_WIDTH), 1) // HEAD_DIM
    for h in range(C_HEADS):
        qh = jnp.where(head == h, q, jnp.zeros_like(q))
        s_ref[h] = _dot_nt(kk, qh)
    rows = HEAD_DIM + SUM_ROWS
    krow = lax.broadcasted_iota(jnp.int32, (nk, tq), 0)
    kneg = jnp.where(krow >= (2 - j) * tq, 0.0, NEG)
    outs = []
    for h in range(C_HEADS):
        s = s_ref[h] + bias_ref[h] + kneg
        m = jnp.max(s, axis=0, keepdims=True)
        p = jnp.exp2(s - m).astype(BF16)
        pv = jnp.dot(vts[0][h * rows:(h + 1) * rows, :], p[0:tq, :], preferred_element_type=F32)
        for t in (1, 2):
            pv = pv + jnp.dot(vts[t][h * rows:(h + 1) * rows, :], p[t * tq:(t + 1) * tq, :],
                              preferred_element_type=F32)
        outs.append(pv[:HEAD_DIM, :] / pv[HEAD_DIM:HEAD_DIM + 1, :])
    out = jnp.concatenate(outs, axis=0).T
    o_ref[:, B_WIDTH:B_WIDTH + C_WIDTH] = out.astype(o_ref.dtype)


def _mix_bc(u, c, vct, rbp, pw, ps, layer, bsz, s_len):
    n = u.shape[0]
    tq = TQ_C
    assert 2 * tq == C_LEFT_CHUNKS * CHUNK and rbp.shape[-1] == 4 * tq
    nt = s_len // tq
    hpt = tq // POOL_HALO
    crows = vct.shape[1]
    cur = lambda col: (lambda b, j: (b * nt + j, col))
    prev = lambda back, col: (lambda b, j: (b * nt + jnp.maximum(j - back, 0), col))
    blk = lambda: (tq, C_WIDTH)
    vblk = lambda back: pl.BlockSpec((None, crows, tq),
                                     lambda b, j: (b * nt + jnp.maximum(j - back, 0), 0, 0))
    return pl.pallas_call(
        functools.partial(_mix_bc_kernel, tq=tq),
        grid=(bsz, nt),
        in_specs=[
            pl.BlockSpec((None,) + rbp.shape[1:], lambda b, j: (layer, 0, 0)),
            pl.BlockSpec((None,) + pw.shape[1:], lambda b, j: (layer, 0, 0)),
            pl.BlockSpec((None, 1, B_WIDTH), lambda b, j: (layer, 0, 0)),
            pl.BlockSpec((tq, B_WIDTH), cur(0)),
            pl.BlockSpec((POOL_HALO, B_WIDTH),
                         lambda b, j: ((b * nt + j) * hpt - jnp.minimum(j, 1), 0)),
            pl.BlockSpec(blk(), cur(0)),
            pl.BlockSpec(blk(), prev(2, 1)), pl.BlockSpec(blk(), prev(1, 1)), pl.BlockSpec(blk(), cur(1)),
            vblk(2), vblk(1), vblk(0),
        ],
        out_specs=pl.BlockSpec((tq, B_WIDTH + C_WIDTH), lambda b, j: (b * nt + j, 0)),
        out_shape=jax.ShapeDtypeStruct((n, B_WIDTH + C_WIDTH), BF16),
        scratch_shapes=[pltpu.VMEM((C_HEADS, 3 * tq, tq), F32),
                        pltpu.VMEM((C_HEADS, 3 * tq, tq), F32)],
        compiler_params=pltpu.CompilerParams(
            dimension_semantics=("arbitrary", "arbitrary"), vmem_limit_bytes=VMEM_LIMIT),
        name="mix_bc",
    )(rbp, pw, ps, u, u, c, c, c, c, vct, vct, vct)


def _post_kernel(x_ref, a_ref, bc_ref, wo_ref, g_ref, wup_ref, cw_ref, cb_ref, wd_ref, gf_ref,
                 o_ref, h_ref, acc_ref, carry_ref, a0_ref, g0_ref, a1_ref, g1_ref, hid_ref,
                 *, tiles_per_seq, final_norm):
    i = pl.program_id(0)
    tm = x_ref.shape[0]
    d_ff = wd_ref.shape[0]
    fc = a0_ref.shape[1]
    nch = d_ff // fc

    def cols(c, base=0):
        start = base + c * fc
        return pl.ds(start if isinstance(start, int) else pl.multiple_of(start, fc), fc)

    xm = (x_ref[...]
          + jnp.dot(a_ref[...], wo_ref[0:A_WIDTH, :], preferred_element_type=F32)
          + jnp.dot(bc_ref[...], wo_ref[A_WIDTH:, :], preferred_element_type=F32))
    h_ref[...] = _rms(xm, g_ref[...]).astype(BF16)
    acc_ref[...] = xm
    seq_start = (i % tiles_per_seq) == 0
    row = lax.broadcasted_iota(jnp.int32, (8, 1), 0)

    def up(c, a_ref, g_ref):
        h = h_ref[...]
        a_ref[...] = jnp.dot(h, wup_ref[:, cols(c)], preferred_element_type=F32)
        g_ref[...] = jnp.dot(h, wup_ref[:, cols(c, d_ff)], preferred_element_type=F32)

    def params(c):
        tail = jnp.where(seq_start, 0.0, carry_ref[c])
        return tail, cw_ref[:, cols(c)], cb_ref[:, cols(c)]

    def gate(c, a_ref, g_ref, prm):
        tail, cw, cb = prm
        a = a_ref[...]
        carry_ref[c] = a[tm - 8:, :]
        r1 = pltpu.roll(a, 1, axis=0)
        r2 = pltpu.roll(a, 2, axis=0)
        top1 = jnp.where(row == 0, tail[7:8, :], r1[:8, :])
        top2 = jnp.where(row == 0, tail[6:7, :], jnp.where(row == 1, tail[7:8, :], r2[:8, :]))
        a1 = jnp.concatenate([top1, r1[8:, :]], axis=0)
        a2 = jnp.concatenate([top2, r2[8:, :]], axis=0)
        conv = cb + cw[0:1, :] * a2
        conv = conv + cw[1:2, :] * a1
        conv = conv + cw[2:3, :] * a
        hid_ref[:, cols(c)] = (conv * jax.nn.sigmoid(conv) * g_ref[...]).astype(BF16)

    assert nch % 2 == 1 and nch >= 3
    up(0, a0_ref, g0_ref)
    up(1, a1_ref, g1_ref)

    def pair(pp, carry):
        c = 2 * pp
        prm0, prm1 = params(c), params(c + 1)
        gate(c, a0_ref, g0_ref, prm0)
        up(c + 2, a0_ref, g0_ref)
        gate(c + 1, a1_ref, g1_ref, prm1)
        up(c + 3, a1_ref, g1_ref)
        return carry

    npair = (nch - 3) // 2
    assert npair % 2 == 0
    lax.fori_loop(0, npair // 2, lambda qq, carry: pair(2 * qq + 1, pair(2 * qq, carry)), 0)
    gate(nch - 3, a0_ref, g0_ref, params(nch - 3))
    up(nch - 1, a0_ref, g0_ref)
    gate(nch - 2, a1_ref, g1_ref, params(nch - 2))
    gate(nch - 1, a0_ref, g0_ref, params(nch - 1))
    out = acc_ref[...] + jnp.dot(hid_ref[...], wd_ref[...], preferred_element_type=F32)
    if final_norm:
        out = _rms(out, gf_ref[...])
    o_ref[...] = out


def _post(x2, oa, obc, wo, ffn_norm, wup, cw, cb, wd, gf, layer, s_len, final_norm):
    n, d = x2.shape
    tm, fc = TM_POST, FF_CHUNK
    d_ff = wd.shape[1]
    nch = d_ff // fc
    const = pl.Buffered(1)
    l3 = lambda i: (layer, 0, 0)
    return pl.pallas_call(
        functools.partial(_post_kernel, tiles_per_seq=s_len // tm, final_norm=final_norm),
        grid=(n // tm,),
        in_specs=[
            pl.BlockSpec((tm, d), lambda i: (i, 0)),
            pl.BlockSpec((tm, A_WIDTH), lambda i: (i, 0)),
            pl.BlockSpec((tm, B_WIDTH + C_WIDTH), lambda i: (i, 0)),
            pl.BlockSpec((None,) + wo.shape[1:], l3, pipeline_mode=const),
            pl.BlockSpec((None, 1, d), l3),
            pl.BlockSpec((None,) + wup.shape[1:], l3, pipeline_mode=const),
            pl.BlockSpec((None,) + cw.shape[1:], l3),
            pl.BlockSpec((None,) + cb.shape[1:], l3),
            pl.BlockSpec((None,) + wd.shape[1:], l3, pipeline_mode=const),
            pl.BlockSpec((1, d), lambda i: (0, 0)),
        ],
        out_specs=pl.BlockSpec((tm, d), lambda i: (i, 0)),
        out_shape=jax.ShapeDtypeStruct((n, d), F32),
        scratch_shapes=[
            pltpu.VMEM((tm, d), BF16),
            pltpu.VMEM((tm, d), F32),
            pltpu.VMEM((nch, 8, fc), F32),
            pltpu.VMEM((tm, fc), F32), pltpu.VMEM((tm, fc), F32),
            pltpu.VMEM((tm, fc), F32), pltpu.VMEM((tm, fc), F32),
            pltpu.VMEM((tm, d_ff), BF16),
        ],
        compiler_params=pltpu.CompilerParams(
            dimension_semantics=("arbitrary",), vmem_limit_bytes=VMEM_LIMIT),
        name="post",
    )(x2, oa, obc, wo, ffn_norm, wup, cw, cb, wd, gf)


def kernel(x, attn_norm, w_in, lam_q1, lam_k1, lam_q2, lam_k2, diff_subln, pool_w, pool_scale,
           rel_bias, w_out, ffn_norm, w_up, conv_w, conv_b, w_down, final_norm):
    bsz, s_len, d = x.shape
    depth = w_in.shape[0]
    d_ff = w_down.shape[1]
    nch = d_ff // FF_CHUNK
    assert nch * FF_CHUNK == d_ff and s_len % TQ_A == 0 and s_len % TM_POST == 0

    w_in_b = w_in.astype(BF16)
    wo = w_out.astype(BF16)
    wup = w_up.astype(BF16)
    wd = w_down.astype(BF16)
    cw = conv_w
    cb = conv_b[:, None, :]
    lamvec = jnp.stack([lam_q1, lam_k1, lam_q2, lam_k2], axis=1)
    gain = diff_subln[:, :, None]
    eye = jnp.eye(B_GROUPS, dtype=pool_w.dtype)
    pw = jnp.einsum('lgcd,gh->lgchd', pool_w, eye).reshape(depth, B_WIDTH, B_WIDTH).astype(BF16)
    ps = pool_scale[:, None, :]
    rbp = jnp.pad(rel_bias, ((0, 0), (0, 0), (0, 4 * TQ_C - rel_bias.shape[-1])))
    an = attn_norm[:, None, :]
    fn = ffn_norm[:, None, :]
    gf = final_norm[None, :]

    x2 = x.reshape(bsz * s_len, d)
    for layer in range(depth):
        qk, vt, u, c, vct = _in_proj(x2, an, w_in_b, layer)
        oa = _attn_a(qk, vt, lamvec, gain, layer, bsz, s_len)
        obc = _mix_bc(u, c, vct, rbp, pw, ps, layer, bsz, s_len)
        x2 = _post(x2, oa, obc, wo, fn, wup, cw, cb, wd, gf, layer, s_len,
                   final_norm=(layer == depth - 1))
    return x2.reshape(bsz, s_len, d)
```

```python
import functools
import math

import jax
import jax.numpy as jnp
from jax import lax
from jax.experimental import pallas as pl
from jax.experimental.pallas import tpu as pltpu

F32 = jnp.float32
BF16 = jnp.bfloat16

CHUNK = 64
HEAD_DIM = 64
A_HEADS = 4
A_V = 2 * HEAD_DIM
A_WIDTH = A_HEADS * A_V
B_GROUPS = 4
B_GROUP_DIM = 64
B_WIDTH = B_GROUPS * B_GROUP_DIM
POOL_WINDOWS = (2, 4, 8, 16)
C_HEADS = 4
C_WIDTH = C_HEADS * HEAD_DIM
C_LEFT_CHUNKS = 8
REL_CLIP = 256
EPS = 1e-5
NEG = -1e30
LOG2E = math.log2(math.e)
SUM_ROWS = 16

TM_PROJ = 512
TQ_A = 2048
TK_A = 512
STRIP_A = 512
DIAG_SUB = 256
TQ_C = 256
TM_POST = 512
FF_CHUNK = 256
POOL_HALO = 16
VMEM_LIMIT = 56 * 1024 * 1024


def _rms(x, g):
    ms = jnp.mean(x * x, axis=-1, keepdims=True)
    return x * lax.rsqrt(ms + EPS) * g


def _dot_nt(a, b):
    return lax.dot_general(a, b, (((1,), (1,)), ((), ())), preferred_element_type=F32)


def _in_proj_kernel(x_ref, g_ref, w_ref, qk_ref, vt_ref, u_ref, c_ref, vct_ref, wvt_ref, wvct_ref,
                    *, tk, tc):
    v0 = 2 * A_WIDTH
    u0 = 3 * A_WIDTH
    c0 = u0 + B_WIDTH
    vc0 = c0 + 2 * C_WIDTH

    @pl.when(pl.program_id(0) == 0)
    def _transpose_value_weights():
        wvt_ref[...] = w_ref[:, v0:u0].astype(F32).T.astype(BF16)
        wvct_ref[...] = w_ref[:, vc0:].astype(F32).T.astype(BF16)

    h = _rms(x_ref[...], g_ref[...]).astype(BF16)
    qk = jnp.dot(h, w_ref[:, :v0], preferred_element_type=F32)
    col = lax.broadcasted_iota(jnp.int32, (1, qk.shape[1]), 1)
    qscale = HEAD_DIM ** -0.5 * LOG2E
    qk_ref[...] = (qk * jnp.where(col < A_WIDTH, qscale, 1.0)).astype(BF16)
    vt = _dot_nt(wvt_ref[...], h).astype(BF16)
    rows = A_V + SUM_ROWS
    for t in range(vt_ref.shape[0]):
        for hd in range(A_HEADS):
            vt_ref[t, hd * rows:hd * rows + A_V, :] = vt[hd * A_V:(hd + 1) * A_V, t * tk:(t + 1) * tk]
            vt_ref[t, hd * rows + A_V:(hd + 1) * rows, :] = jnp.ones((SUM_ROWS, tk), BF16)
    u_ref[...] = jnp.dot(h, w_ref[:, u0:c0], preferred_element_type=F32)
    c = jnp.dot(h, w_ref[:, c0:vc0], preferred_element_type=F32)
    colc = lax.broadcasted_iota(jnp.int32, (1, c.shape[1]), 1)
    c_ref[...] = (c * jnp.where(colc < C_WIDTH, qscale, 1.0)).astype(BF16)
    vct = _dot_nt(wvct_ref[...], h).astype(BF16)
    rows = HEAD_DIM + SUM_ROWS
    for t in range(vct_ref.shape[0]):
        for hd in range(C_HEADS):
            vct_ref[t, hd * rows:hd * rows + HEAD_DIM, :] = (
                vct[hd * HEAD_DIM:(hd + 1) * HEAD_DIM, t * tc:(t + 1) * tc])
            vct_ref[t, hd * rows + HEAD_DIM:(hd + 1) * rows, :] = jnp.ones((SUM_ROWS, tc), BF16)


def _in_proj(x2, attn_norm, w_in_b, layer):
    n, d = x2.shape
    tm, tk, tc = TM_PROJ, TK_A, TQ_C
    lsel = lambda *_: (layer, 0, 0)
    arows = A_HEADS * (A_V + SUM_ROWS)
    crows = C_HEADS * (HEAD_DIM + SUM_ROWS)
    return pl.pallas_call(
        functools.partial(_in_proj_kernel, tk=tk, tc=tc),
        grid=(n // tm,),
        in_specs=[
            pl.BlockSpec((tm, d), lambda i: (i, 0)),
            pl.BlockSpec((None, 1, d), lsel),
            pl.BlockSpec((None,) + w_in_b.shape[1:], lsel),
        ],
        out_specs=[
            pl.BlockSpec((tm, 2 * A_WIDTH), lambda i: (i, 0)),
            pl.BlockSpec((tm // tk, arows, tk), lambda i: (i, 0, 0)),
            pl.BlockSpec((tm, B_WIDTH), lambda i: (i, 0)),
            pl.BlockSpec((tm, 2 * C_WIDTH), lambda i: (i, 0)),
            pl.BlockSpec((tm // tc, crows, tc), lambda i: (i, 0, 0)),
        ],
        out_shape=[
            jax.ShapeDtypeStruct((n, 2 * A_WIDTH), BF16),
            jax.ShapeDtypeStruct((n // tk, arows, tk), BF16),
            jax.ShapeDtypeStruct((n, B_WIDTH), F32),
            jax.ShapeDtypeStruct((n, 2 * C_WIDTH), BF16),
            jax.ShapeDtypeStruct((n // tc, crows, tc), BF16),
        ],
        scratch_shapes=[
            pltpu.VMEM((A_WIDTH, d), BF16),
            pltpu.VMEM((C_WIDTH, d), BF16),
        ],
        compiler_params=pltpu.CompilerParams(
            dimension_semantics=("arbitrary",), vmem_limit_bytes=VMEM_LIMIT),
        name="in_proj",
    )(x2, attn_norm, w_in_b)


def _attn_a_kernel(lam_ref, gain_ref, q_ref, k_ref, vt_ref, o_ref,
                   qcat_ref, sa_ref, sb_ref, m_ref, l_ref, acc_ref,
                   *, tq, tk, strip, nq, lam_init):
    nstrip = 2 * tq // strip

    def load_queries(i):
        q = q_ref[pl.ds(pl.multiple_of(i * tq, tq), tq), :]
        lane = lax.broadcasted_iota(jnp.int32, q.shape, 1)
        zero = jnp.zeros_like(q)
        qcat_ref[0:tq, :] = jnp.where(lane < HEAD_DIM, q, zero)
        qcat_ref[tq:2 * tq, :] = jnp.where(lane >= HEAD_DIM, q, zero)

    def reset_state():
        m_ref[...] = jnp.full(m_ref.shape, NEG, F32)
        l_ref[...] = jnp.zeros(l_ref.shape, F32)
        acc_ref[...] = jnp.zeros(acc_ref.shape, F32)

    def visible(c, diag):
        qblock = ((c * strip) % tq) // tk
        return 2 if diag is None or qblock > diag else int(qblock == diag)

    def pieces(c, diag):
        kind = visible(c, diag)
        if kind == 0:
            return []
        if kind == 2:
            return [(c * strip, strip, tk, False)]
        out = []
        for h in range(strip // DIAG_SUB):
            off = (c * strip) % tk + h * DIAG_SUB
            out.append((c * strip + h * DIAG_SUB, DIAG_SUB, off + DIAG_SUB, True))
        return out

    def scores(j, c, dst_ref, diag=None):
        for col, width, keys, _ in pieces(c, diag):
            k_t = k_ref[pl.ds(pl.multiple_of(j * tk, tk), keys), :]
            dst_ref[0:keys, col:col + width] = _dot_nt(k_t, qcat_ref[col:col + width, :])

    def consume(t, c, src_ref, diag=None):
        for col, width, keys, masked in pieces(c, diag):
            cs = slice(col, col + width)
            s = src_ref[0:keys, cs]
            if masked:
                kc = lax.broadcasted_iota(jnp.int32, s.shape, 0) // CHUNK
                qc = (col % tk + lax.broadcasted_iota(jnp.int32, s.shape, 1)) // CHUNK
                s = jnp.where(kc <= qc, s, NEG)
            m_old = m_ref[:, cs]
            m_new = jnp.maximum(m_old, jnp.max(s, axis=0, keepdims=True))
            alpha = jnp.exp2(m_old - m_new)
            p = jnp.exp2(s - m_new).astype(BF16)
            pv = jnp.dot(vt_ref[t, :, 0:keys], p, preferred_element_type=F32)
            l_ref[:, cs] = alpha * l_ref[:, cs] + pv[A_V:A_V + 1, :]
            acc_ref[:, cs] = alpha * acc_ref[:, cs] + pv[:A_V, :]
            m_ref[:, cs] = m_new

    def pipelined(j, dst_ref, src_ref, diag_next=None, diag=None):
        for c in range(nstrip):
            scores(j, c, dst_ref, diag_next)
            consume(j - 1, c, src_ref, diag)

    def finalize(i):
        lv = lam_ref[...]
        lam = (jnp.exp(jnp.sum(lv[0:1] * lv[1:2], axis=1, keepdims=True))
               - jnp.exp(jnp.sum(lv[2:3] * lv[3:4], axis=1, keepdims=True)) + lam_init)
        inv = 1.0 / l_ref[...]
        a = (acc_ref[:, :tq] * inv[:, :tq]
             - acc_ref[:, tq:] * (lam * inv[:, tq:]))
        ms = jnp.mean(a * a, axis=0, keepdims=True)
        y = a * lax.rsqrt(ms + EPS) * (gain_ref[...] * (1.0 - lam_init))
        o_ref[pl.ds(pl.multiple_of(i * tq, tq), tq), :] = y.T.astype(o_ref.dtype)

    def pair(mm, carry):
        pipelined(2 * mm - 1, sb_ref, sa_ref)
        pipelined(2 * mm, sa_ref, sb_ref)
        return carry

    ratio = tq // tk
    assert ratio % 2 == 0
    load_queries(0)
    reset_state()
    for c in range(nstrip):
        scores(0, c, sa_ref)

    def query_tile(i, carry):
        first = ratio * i
        lax.fori_loop(1, first // 2 + 1, pair, 0)
        bufs = (sa_ref, sb_ref)
        for r in range(ratio - 1):
            pipelined(first + r + 1, bufs[(r + 1) % 2], bufs[r % 2], diag_next=r + 1, diag=r)
        load_queries(jnp.minimum(i + 1, nq - 1))
        for c in range(nstrip):
            scores(0, c, sa_ref)
            consume(first + ratio - 1, c, sb_ref, diag=ratio - 1)
        finalize(i)
        reset_state()
        return carry

    lax.fori_loop(0, nq, query_tile, 0)


def _attn_a(qk, vt, lamvec, gain, layer, bsz, s_len):
    n = qk.shape[0]
    tq, tk = TQ_A, TK_A
    assert tq % (2 * tk) == 0 and tk % STRIP_A == 0
    nq = s_len // tq
    lam_init = 0.8 - 0.6 * math.exp(-0.3 * layer)
    return pl.pallas_call(
        functools.partial(_attn_a_kernel, tq=tq, tk=tk, strip=STRIP_A, nq=nq, lam_init=lam_init),
        grid=(bsz, A_HEADS),
        in_specs=[
            pl.BlockSpec((None, 4, HEAD_DIM), lambda b, h: (layer, 0, 0)),
            pl.BlockSpec((None, A_V, 1), lambda b, h: (layer, 0, 0)),
            pl.BlockSpec((s_len, A_V), lambda b, h: (b, h)),
            pl.BlockSpec((s_len, A_V), lambda b, h: (b, A_HEADS + h)),
            pl.BlockSpec((s_len // tk, A_V + SUM_ROWS, tk), lambda b, h: (b, h, 0)),
        ],
        out_specs=pl.BlockSpec((s_len, A_V), lambda b, h: (b, h)),
        out_shape=jax.ShapeDtypeStruct((n, A_WIDTH), BF16),
        scratch_shapes=[
            pltpu.VMEM((2 * tq, A_V), BF16),
            pltpu.VMEM((tk, 2 * tq), F32),
            pltpu.VMEM((tk, 2 * tq), F32),
            pltpu.VMEM((1, 2 * tq), F32),
            pltpu.VMEM((1, 2 * tq), F32),
            pltpu.VMEM((A_V, 2 * tq), F32),
        ],
        compiler_params=pltpu.CompilerParams(
            dimension_semantics=("arbitrary", "arbitrary"),
            vmem_limit_bytes=VMEM_LIMIT),
        name="attn_a",
    )(lamvec, gain, qk, qk, vt)


def _mix_bc_kernel(rb_ref, pw_ref, ps_ref, u_ref, up_ref, q_ref,
                   k0_ref, k1_ref, k2_ref, v0_ref, v1_ref, v2_ref,
                   o_ref, bias_ref, s_ref, *, tq):
    j = pl.program_id(1)
    nk = 3 * tq

    @pl.when((pl.program_id(0) == 0) & (j == 0))
    def _build_bias():
        rb = rb_ref[...]
        lane = lax.broadcasted_iota(jnp.int32, rb.shape, 1)
        edge = rb[:, 2 * REL_CLIP:2 * REL_CLIP + 1]
        ext = jnp.where(lane > 2 * REL_CLIP, edge, rb)
        kc = lax.broadcasted_iota(jnp.int32, (nk, tq), 0) // CHUNK
        qc = lax.broadcasted_iota(jnp.int32, (nk, tq), 1) // CHUNK
        band = (kc >= qc) & (kc <= qc + C_LEFT_CHUNKS)
        for h in range(C_HEADS):
            rows = jnp.broadcast_to(ext[h:h + 1, :], (nk, ext.shape[1]))
            rolled = pltpu.roll(rows, REL_CLIP, axis=1, stride=1, stride_axis=0)
            bias_ref[h] = jnp.where(band, rolled[:, :tq] * LOG2E, NEG)

    u = u_ref[...]
    halo = jnp.where(j > 0, up_ref[...], 0.0)
    ext_u = jnp.concatenate([halo, u], axis=0)
    s2 = ext_u + pltpu.roll(ext_u, 1, axis=0)
    s4 = s2 + pltpu.roll(s2, 2, axis=0)
    s8 = s4 + pltpu.roll(s4, 4, axis=0)
    s16 = s8 + pltpu.roll(s8, 8, axis=0)
    grp = lax.broadcasted_iota(jnp.int32, (1, B_WIDTH), 1) // B_GROUP_DIM
    wsum = jnp.where(grp == 0, s2, jnp.where(grp == 1, s4, jnp.where(grp == 2, s8, s16)))
    wsum = wsum[POOL_HALO:, :]
    win = jnp.where(grp == 0, 2, jnp.where(grp == 1, 4, jnp.where(grp == 2, 8, 16)))
    pos = j * tq + lax.broadcasted_iota(jnp.int32, (tq, 1), 0)
    cnt = jnp.minimum(pos + 1, win).astype(F32)
    d = wsum / cnt - u
    yb = jnp.dot(d.astype(BF16), pw_ref[...], preferred_element_type=F32) * ps_ref[...]
    o_ref[:, 0:B_WIDTH] = yb.astype(o_ref.dtype)

    q = q_ref[...]
    kk = jnp.concatenate([k0_ref[...], k1_ref[...], k2_ref[...]], axis=0)
    vts = (v0_ref, v1_ref, v2_ref)
    head = lax.broadcasted_iota(jnp.int32, (1, C_WIDTH), 1) // HEAD_DIM
    for h in range(C_HEADS):
        qh = jnp.where(head == h, q, jnp.zeros_like(q))
        s_ref[h] = _dot_nt(kk, qh)
    rows = HEAD_DIM + SUM_ROWS
    krow = lax.broadcasted_iota(jnp.int32, (nk, tq), 0)
    kneg = jnp.where(krow >= (2 - j) * tq, 0.0, NEG)
    outs = []
    for h in range(C_HEADS):
        s = s_ref[h] + bias_ref[h] + kneg
        m = jnp.max(s, axis=0, keepdims=True)
        p = jnp.exp2(s - m).astype(BF16)
        pv = jnp.dot(vts[0][h * rows:(h + 1) * rows, :], p[0:tq, :], preferred_element_type=F32)
        for t in (1, 2):
            pv = pv + jnp.dot(vts[t][h * rows:(h + 1) * rows, :], p[t * tq:(t + 1) * tq, :],
                              preferred_element_type=F32)
        outs.append(pv[:HEAD_DIM, :] / pv[HEAD_DIM:HEAD_DIM + 1, :])
    out = jnp.concatenate(outs, axis=0).T
    o_ref[:, B_WIDTH:B_WIDTH + C_WIDTH] = out.astype(o_ref.dtype)


def _mix_bc(u, c, vct, rbp, pw, ps, layer, bsz, s_len):
    n = u.shape[0]
    tq = TQ_C
    assert 2 * tq == C_LEFT_CHUNKS * CHUNK and rbp.shape[-1] == 4 * tq
    nt = s_len // tq
    hpt = tq // POOL_HALO
    crows = vct.shape[1]
    cur = lambda col: (lambda b, j: (b * nt + j, col))
    prev = lambda back, col: (lambda b, j: (b * nt + jnp.maximum(j - back, 0), col))
    blk = lambda: (tq, C_WIDTH)
    vblk = lambda back: pl.BlockSpec((None, crows, tq),
                                     lambda b, j: (b * nt + jnp.maximum(j - back, 0), 0, 0))
    return pl.pallas_call(
        functools.partial(_mix_bc_kernel, tq=tq),
        grid=(bsz, nt),
        in_specs=[
            pl.BlockSpec((None,) + rbp.shape[1:], lambda b, j: (layer, 0, 0)),
            pl.BlockSpec((None,) + pw.shape[1:], lambda b, j: (layer, 0, 0)),
            pl.BlockSpec((None, 1, B_WIDTH), lambda b, j: (layer, 0, 0)),
            pl.BlockSpec((tq, B_WIDTH), cur(0)),
            pl.BlockSpec((POOL_HALO, B_WIDTH),
                         lambda b, j: ((b * nt + j) * hpt - jnp.minimum(j, 1), 0)),
            pl.BlockSpec(blk(), cur(0)),
            pl.BlockSpec(blk(), prev(2, 1)), pl.BlockSpec(blk(), prev(1, 1)), pl.BlockSpec(blk(), cur(1)),
            vblk(2), vblk(1), vblk(0),
        ],
        out_specs=pl.BlockSpec((tq, B_WIDTH + C_WIDTH), lambda b, j: (b * nt + j, 0)),
        out_shape=jax.ShapeDtypeStruct((n, B_WIDTH + C_WIDTH), BF16),
        scratch_shapes=[pltpu.VMEM((C_HEADS, 3 * tq, tq), F32),
                        pltpu.VMEM((C_HEADS, 3 * tq, tq), F32)],
        compiler_params=pltpu.CompilerParams(
            dimension_semantics=("arbitrary", "arbitrary"), vmem_limit_bytes=VMEM_LIMIT),
        name="mix_bc",
    )(rbp, pw, ps, u, u, c, c, c, c, vct, vct, vct)


def _post_kernel(x_ref, a_ref, bc_ref, wo_ref, g_ref, wup_ref, cw_ref, cb_ref, wd_ref, gf_ref,
                 o_ref, h_ref, acc_ref, carry_ref, a0_ref, g0_ref, a1_ref, g1_ref, hid_ref,
                 *, tiles_per_seq, final_norm):
    i = pl.program_id(0)
    tm = x_ref.shape[0]
    d_ff = wd_ref.shape[0]
    fc = a0_ref.shape[1]
    nch = d_ff // fc

    def cols(c, base=0):
        start = base + c * fc
        return pl.ds(start if isinstance(start, int) else pl.multiple_of(start, fc), fc)

    xm = (x_ref[...]
          + jnp.dot(a_ref[...], wo_ref[0:A_WIDTH, :], preferred_element_type=F32)
          + jnp.dot(bc_ref[...], wo_ref[A_WIDTH:, :], preferred_element_type=F32))
    h_ref[...] = _rms(xm, g_ref[...]).astype(BF16)
    acc_ref[...] = xm
    seq_start = (i % tiles_per_seq) == 0
    row = lax.broadcasted_iota(jnp.int32, (8, 1), 0)

    def up(c, a_ref, g_ref):
        h = h_ref[...]
        a_ref[...] = jnp.dot(h, wup_ref[:, cols(c)], preferred_element_type=F32)
        g_ref[...] = jnp.dot(h, wup_ref[:, cols(c, d_ff)], preferred_element_type=F32)

    def params(c):
        tail = jnp.where(seq_start, 0.0, carry_ref[c])
        return tail, cw_ref[:, cols(c)], cb_ref[:, cols(c)]

    def gate(c, a_ref, g_ref, prm):
        tail, cw, cb = prm
        a = a_ref[...]
        carry_ref[c] = a[tm - 8:, :]
        r1 = pltpu.roll(a, 1, axis=0)
        r2 = pltpu.roll(a, 2, axis=0)
        top1 = jnp.where(row == 0, tail[7:8, :], r1[:8, :])
        top2 = jnp.where(row == 0, tail[6:7, :], jnp.where(row == 1, tail[7:8, :], r2[:8, :]))
        a1 = jnp.concatenate([top1, r1[8:, :]], axis=0)
        a2 = jnp.concatenate([top2, r2[8:, :]], axis=0)
        conv = cb + cw[0:1, :] * a2
        conv = conv + cw[1:2, :] * a1
        conv = conv + cw[2:3, :] * a
        hid_ref[:, cols(c)] = (conv * jax.nn.sigmoid(conv) * g_ref[...]).astype(BF16)

    assert nch % 2 == 1 and nch >= 3
    up(0, a0_ref, g0_ref)
    up(1, a1_ref, g1_ref)

    def pair(pp, carry):
        c = 2 * pp
        prm0, prm1 = params(c), params(c + 1)
        gate(c, a0_ref, g0_ref, prm0)
        up(c + 2, a0_ref, g0_ref)
        gate(c + 1, a1_ref, g1_ref, prm1)
        up(c + 3, a1_ref, g1_ref)
        return carry

    npair = (nch - 3) // 2
    assert npair % 2 == 0
    lax.fori_loop(0, npair // 2, lambda qq, carry: pair(2 * qq + 1, pair(2 * qq, carry)), 0)
    gate(nch - 3, a0_ref, g0_ref, params(nch - 3))
    up(nch - 1, a0_ref, g0_ref)
    gate(nch - 2, a1_ref, g1_ref, params(nch - 2))
    gate(nch - 1, a0_ref, g0_ref, params(nch - 1))
    out = acc_ref[...] + jnp.dot(hid_ref[...], wd_ref[...], preferred_element_type=F32)
    if final_norm:
        out = _rms(out, gf_ref[...])
    o_ref[...] = out


def _post(x2, oa, obc, wo, ffn_norm, wup, cw, cb, wd, gf, layer, s_len, final_norm):
    n, d = x2.shape
    tm, fc = TM_POST, FF_CHUNK
    d_ff = wd.shape[1]
    nch = d_ff // fc
    const = pl.Buffered(1)
    l3 = lambda i: (layer, 0, 0)
    return pl.pallas_call(
        functools.partial(_post_kernel, tiles_per_seq=s_len // tm, final_norm=final_norm),
        grid=(n // tm,),
        in_specs=[
            pl.BlockSpec((tm, d), lambda i: (i, 0)),
            pl.BlockSpec((tm, A_WIDTH), lambda i: (i, 0)),
            pl.BlockSpec((tm, B_WIDTH + C_WIDTH), lambda i: (i, 0)),
            pl.BlockSpec((None,) + wo.shape[1:], l3, pipeline_mode=const),
            pl.BlockSpec((None, 1, d), l3),
            pl.BlockSpec((None,) + wup.shape[1:], l3, pipeline_mode=const),
            pl.BlockSpec((None,) + cw.shape[1:], l3),
            pl.BlockSpec((None,) + cb.shape[1:], l3),
            pl.BlockSpec((None,) + wd.shape[1:], l3, pipeline_mode=const),
            pl.BlockSpec((1, d), lambda i: (0, 0)),
        ],
        out_specs=pl.BlockSpec((tm, d), lambda i: (i, 0)),
        out_shape=jax.ShapeDtypeStruct((n, d), F32),
        scratch_shapes=[
            pltpu.VMEM((tm, d), BF16),
            pltpu.VMEM((tm, d), F32),
            pltpu.VMEM((nch, 8, fc), F32),
            pltpu.VMEM((tm, fc), F32), pltpu.VMEM((tm, fc), F32),
            pltpu.VMEM((tm, fc), F32), pltpu.VMEM((tm, fc), F32),
            pltpu.VMEM((tm, d_ff), BF16),
        ],
        compiler_params=pltpu.CompilerParams(
            dimension_semantics=("arbitrary",), vmem_limit_bytes=VMEM_LIMIT),
        name="post",
    )(x2, oa, obc, wo, ffn_norm, wup, cw, cb, wd, gf)


def kernel(x, attn_norm, w_in, lam_q1, lam_k1, lam_q2, lam_k2, diff_subln, pool_w, pool_scale,
           rel_bias, w_out, ffn_norm, w_up, conv_w, conv_b, w_down, final_norm):
    bsz, s_len, d = x.shape
    depth = w_in.shape[0]
    d_ff = w_down.shape[1]
    nch = d_ff // FF_CHUNK
    assert nch * FF_CHUNK == d_ff and s_len % TQ_A == 0 and s_len % TM_POST == 0

    w_in_b = w_in.astype(BF16)
    wo = w_out.astype(BF16)
    wup = w_up.astype(BF16)
    wd = w_down.astype(BF16)
    cw = conv_w
    cb = conv_b[:, None, :]
    lamvec = jnp.stack([lam_q1, lam_k1, lam_q2, lam_k2], axis=1)
    gain = diff_subln[:, :, None]
    eye = jnp.eye(B_GROUPS, dtype=pool_w.dtype)
    pw = jnp.einsum('lgcd,gh->lgchd', pool_w, eye).reshape(depth, B_WIDTH, B_WIDTH).astype(BF16)
    ps = pool_scale[:, None, :]
    rbp = jnp.pad(rel_bias, ((0, 0), (0, 0), (0, 4 * TQ_C - rel_bias.shape[-1])))
    an = attn_norm[:, None, :]
    fn = ffn_norm[:, None, :]
    gf = final_norm[None, :]

    x2 = x.reshape(bsz * s_len, d)
    for layer in range(depth):
        qk, vt, u, c, vct = _in_proj(x2, an, w_in_b, layer)
        oa = _attn_a(qk, vt, lamvec, gain, layer, bsz, s_len)
        obc = _mix_bc(u, c, vct, rbp, pw, ps, layer, bsz, s_len)
        x2 = _post(x2, oa, obc, wo, fn, wup, cw, cb, wd, gf, layer, s_len,
                   final_norm=(layer == depth - 1))
    return x2.reshape(bsz, s_len, d)
```

```python
import functools
import math

import jax
import jax.numpy as jnp
from jax import lax
from jax.experimental import pallas as pl
from jax.experimental.pallas import tpu as pltpu

F32 = jnp.float32
BF16 = jnp.bfloat16

CHUNK = 64
HEAD_DIM = 64
A_HEADS = 4
A_V = 2 * HEAD_DIM
A_WIDTH = A_HEADS * A_V
B_GROUPS = 4
B_GROUP_DIM = 64
B_WIDTH = B_GROUPS * B_GROUP_DIM
POOL_WINDOWS = (2, 4, 8, 16)
C_HEADS = 4
C_WIDTH = C_HEADS * HEAD_DIM
C_LEFT_CHUNKS = 8
REL_CLIP = 256
EPS = 1e-5
NEG = -1e30
LOG2E = math.log2(math.e)
SUM_ROWS = 16

TM_PROJ = 512
TQ_A = 2048
TK_A = 512
STRIP_A = 512
DIAG_SUB = 256
TQ_C = 256
TM_POST = 512
FF_CHUNK = 256
POOL_HALO = 16
VMEM_LIMIT = 56 * 1024 * 1024


def _rms(x, g):
    ms = jnp.mean(x * x, axis=-1, keepdims=True)
    return x * lax.rsqrt(ms + EPS) * g


def _dot_nt(a, b):
    return lax.dot_general(a, b, (((1,), (1,)), ((), ())), preferred_element_type=F32)


def _in_proj_kernel(x_ref, g_ref, w_ref, qk_ref, vt_ref, u_ref, c_ref, vct_ref, wvt_ref, wvct_ref,
                    *, tk, tc):
    v0 = 2 * A_WIDTH
    u0 = 3 * A_WIDTH
    c0 = u0 + B_WIDTH
    vc0 = c0 + 2 * C_WIDTH

    @pl.when(pl.program_id(0) == 0)
    def _transpose_value_weights():
        wvt_ref[...] = w_ref[:, v0:u0].astype(F32).T.astype(BF16)
        wvct_ref[...] = w_ref[:, vc0:].astype(F32).T.astype(BF16)

    h = _rms(x_ref[...], g_ref[...]).astype(BF16)
    qk = jnp.dot(h, w_ref[:, :v0], preferred_element_type=F32)
    col = lax.broadcasted_iota(jnp.int32, (1, qk.shape[1]), 1)
    qscale = HEAD_DIM ** -0.5 * LOG2E
    qk_ref[...] = (qk * jnp.where(col < A_WIDTH, qscale, 1.0)).astype(BF16)
    vt = _dot_nt(wvt_ref[...], h).astype(BF16)
    rows = A_V + SUM_ROWS
    for t in range(vt_ref.shape[0]):
        for hd in range(A_HEADS):
            vt_ref[t, hd * rows:hd * rows + A_V, :] = vt[hd * A_V:(hd + 1) * A_V, t * tk:(t + 1) * tk]
            vt_ref[t, hd * rows + A_V:(hd + 1) * rows, :] = jnp.ones((SUM_ROWS, tk), BF16)
    u_ref[...] = jnp.dot(h, w_ref[:, u0:c0], preferred_element_type=F32)
    c = jnp.dot(h, w_ref[:, c0:vc0], preferred_element_type=F32)
    colc = lax.broadcasted_iota(jnp.int32, (1, c.shape[1]), 1)
    c_ref[...] = (c * jnp.where(colc < C_WIDTH, qscale, 1.0)).astype(BF16)
    vct = _dot_nt(wvct_ref[...], h).astype(BF16)
    rows = HEAD_DIM + SUM_ROWS
    for t in range(vct_ref.shape[0]):
        for hd in range(C_HEADS):
            vct_ref[t, hd * rows:hd * rows + HEAD_DIM, :] = (
                vct[hd * HEAD_DIM:(hd + 1) * HEAD_DIM, t * tc:(t + 1) * tc])
            vct_ref[t, hd * rows + HEAD_DIM:(hd + 1) * rows, :] = jnp.ones((SUM_ROWS, tc), BF16)


def _in_proj(x2, attn_norm, w_in_b, layer):
    n, d = x2.shape
    tm, tk, tc = TM_PROJ, TK_A, TQ_C
    lsel = lambda *_: (layer, 0, 0)
    arows = A_HEADS * (A_V + SUM_ROWS)
    crows = C_HEADS * (HEAD_DIM + SUM_ROWS)
    return pl.pallas_call(
        functools.partial(_in_proj_kernel, tk=tk, tc=tc),
        grid=(n // tm,),
        in_specs=[
            pl.BlockSpec((tm, d), lambda i: (i, 0)),
            pl.BlockSpec((None, 1, d), lsel),
            pl.BlockSpec((None,) + w_in_b.shape[1:], lsel),
        ],
        out_specs=[
            pl.BlockSpec((tm, 2 * A_WIDTH), lambda i: (i, 0)),
            pl.BlockSpec((tm // tk, arows, tk), lambda i: (i, 0, 0)),
            pl.BlockSpec((tm, B_WIDTH), lambda i: (i, 0)),
            pl.BlockSpec((tm, 2 * C_WIDTH), lambda i: (i, 0)),
            pl.BlockSpec((tm // tc, crows, tc), lambda i: (i, 0, 0)),
        ],
        out_shape=[
            jax.ShapeDtypeStruct((n, 2 * A_WIDTH), BF16),
            jax.ShapeDtypeStruct((n // tk, arows, tk), BF16),
            jax.ShapeDtypeStruct((n, B_WIDTH), F32),
            jax.ShapeDtypeStruct((n, 2 * C_WIDTH), BF16),
            jax.ShapeDtypeStruct((n // tc, crows, tc), BF16),
        ],
        scratch_shapes=[
            pltpu.VMEM((A_WIDTH, d), BF16),
            pltpu.VMEM((C_WIDTH, d), BF16),
        ],
        compiler_params=pltpu.CompilerParams(
            dimension_semantics=("arbitrary",), vmem_limit_bytes=VMEM_LIMIT),
        name="in_proj",
    )(x2, attn_norm, w_in_b)


def _attn_a_kernel(lam_ref, gain_ref, q_ref, k_ref, vt_ref, o_ref,
                   qcat_ref, sa_ref, sb_ref, m_ref, l_ref, acc_ref,
                   *, tq, tk, strip, nq, lam_init):
    nstrip = 2 * tq // strip

    def load_queries(i):
        q = q_ref[pl.ds(pl.multiple_of(i * tq, tq), tq), :]
        lane = lax.broadcasted_iota(jnp.int32, q.shape, 1)
        zero = jnp.zeros_like(q)
        qcat_ref[0:tq, :] = jnp.where(lane < HEAD_DIM, q, zero)
        qcat_ref[tq:2 * tq, :] = jnp.where(lane >= HEAD_DIM, q, zero)

    def reset_state():
        m_ref[...] = jnp.full(m_ref.shape, NEG, F32)
        l_ref[...] = jnp.zeros(l_ref.shape, F32)
        acc_ref[...] = jnp.zeros(acc_ref.shape, F32)

    def visible(c, diag):
        qblock = ((c * strip) % tq) // tk
        return 2 if diag is None or qblock > diag else int(qblock == diag)

    def pieces(c, diag):
        kind = visible(c, diag)
        if kind == 0:
            return []
        if kind == 2:
            return [(c * strip, strip, tk, False)]
        out = []
        for h in range(strip // DIAG_SUB):
            off = (c * strip) % tk + h * DIAG_SUB
            out.append((c * strip + h * DIAG_SUB, DIAG_SUB, off + DIAG_SUB, True))
        return out

    def scores(j, c, dst_ref, diag=None):
        for col, width, keys, _ in pieces(c, diag):
            k_t = k_ref[pl.ds(pl.multiple_of(j * tk, tk), keys), :]
            dst_ref[0:keys, col:col + width] = _dot_nt(k_t, qcat_ref[col:col + width, :])

    def consume(t, c, src_ref, diag=None):
        for col, width, keys, masked in pieces(c, diag):
            cs = slice(col, col + width)
            s = src_ref[0:keys, cs]
            if masked:
                kc = lax.broadcasted_iota(jnp.int32, s.shape, 0) // CHUNK
                qc = (col % tk + lax.broadcasted_iota(jnp.int32, s.shape, 1)) // CHUNK
                s = jnp.where(kc <= qc, s, NEG)
            m_old = m_ref[:, cs]
            m_new = jnp.maximum(m_old, jnp.max(s, axis=0, keepdims=True))
            alpha = jnp.exp2(m_old - m_new)
            p = jnp.exp2(s - m_new).astype(BF16)
            pv = jnp.dot(vt_ref[t, :, 0:keys], p, preferred_element_type=F32)
            l_ref[:, cs] = alpha * l_ref[:, cs] + pv[A_V:A_V + 1, :]
            acc_ref[:, cs] = alpha * acc_ref[:, cs] + pv[:A_V, :]
            m_ref[:, cs] = m_new

    def pipelined(j, dst_ref, src_ref, diag_next=None, diag=None):
        for c in range(nstrip):
            scores(j, c, dst_ref, diag_next)
            consume(j - 1, c, src_ref, diag)

    def finalize(i):
        lv = lam_ref[...]
        lam = (jnp.exp(jnp.sum(lv[0:1] * lv[1:2], axis=1, keepdims=True))
               - jnp.exp(jnp.sum(lv[2:3] * lv[3:4], axis=1, keepdims=True)) + lam_init)
        inv = 1.0 / l_ref[...]
        a = (acc_ref[:, :tq] * inv[:, :tq]
             - acc_ref[:, tq:] * (lam * inv[:, tq:]))
        ms = jnp.mean(a * a, axis=0, keepdims=True)
        y = a * lax.rsqrt(ms + EPS) * (gain_ref[...] * (1.0 - lam_init))
        o_ref[pl.ds(pl.multiple_of(i * tq, tq), tq), :] = y.T.astype(o_ref.dtype)

    def pair(mm, carry):
        pipelined(2 * mm - 1, sb_ref, sa_ref)
        pipelined(2 * mm, sa_ref, sb_ref)
        return carry

    ratio = tq // tk
    assert ratio % 2 == 0
    load_queries(0)
    reset_state()
    for c in range(nstrip):
        scores(0, c, sa_ref)

    def query_tile(i, carry):
        first = ratio * i
        lax.fori_loop(1, first // 2 + 1, pair, 0)
        bufs = (sa_ref, sb_ref)
        for r in range(ratio - 1):
            pipelined(first + r + 1, bufs[(r + 1) % 2], bufs[r % 2], diag_next=r + 1, diag=r)
        load_queries(jnp.minimum(i + 1, nq - 1))
        for c in range(nstrip):
            scores(0, c, sa_ref)
            consume(first + ratio - 1, c, sb_ref, diag=ratio - 1)
        finalize(i)
        reset_state()
        return carry

    lax.fori_loop(0, nq, query_tile, 0)


def _attn_a(qk, vt, lamvec, gain, layer, bsz, s_len):
    n = qk.shape[0]
    tq, tk = TQ_A, TK_A
    assert tq % (2 * tk) == 0 and tk % STRIP_A == 0
    nq = s_len // tq
    lam_init = 0.8 - 0.6 * math.exp(-0.3 * layer)
    return pl.pallas_call(
        functools.partial(_attn_a_kernel, tq=tq, tk=tk, strip=STRIP_A, nq=nq, lam_init=lam_init),
        grid=(bsz, A_HEADS),
        in_specs=[
            pl.BlockSpec((None, 4, HEAD_DIM), lambda b, h: (layer, 0, 0)),
            pl.BlockSpec((None, A_V, 1), lambda b, h: (layer, 0, 0)),
            pl.BlockSpec((s_len, A_V), lambda b, h: (b, h)),
            pl.BlockSpec((s_len, A_V), lambda b, h: (b, A_HEADS + h)),
            pl.BlockSpec((s_len // tk, A_V + SUM_ROWS, tk), lambda b, h: (b, h, 0)),
        ],
        out_specs=pl.BlockSpec((s_len, A_V), lambda b, h: (b, h)),
        out_shape=jax.ShapeDtypeStruct((n, A_WIDTH), BF16),
        scratch_shapes=[
            pltpu.VMEM((2 * tq, A_V), BF16),
            pltpu.VMEM((tk, 2 * tq), F32),
            pltpu.VMEM((tk, 2 * tq), F32),
            pltpu.VMEM((1, 2 * tq), F32),
            pltpu.VMEM((1, 2 * tq), F32),
            pltpu.VMEM((A_V, 2 * tq), F32),
        ],
        compiler_params=pltpu.CompilerParams(
            dimension_semantics=("arbitrary", "arbitrary"),
            vmem_limit_bytes=VMEM_LIMIT),
        name="attn_a",
    )(lamvec, gain, qk, qk, vt)


def _mix_bc_kernel(rb_ref, pw_ref, ps_ref, u_ref, up_ref, q_ref,
                   k0_ref, k1_ref, k2_ref, v0_ref, v1_ref, v2_ref,
                   o_ref, bias_ref, s_ref, *, tq):
    j = pl.program_id(1)
    nk = 3 * tq

    @pl.when((pl.program_id(0) == 0) & (j == 0))
    def _build_bias():
        rb = rb_ref[...]
        lane = lax.broadcasted_iota(jnp.int32, rb.shape, 1)
        edge = rb[:, 2 * REL_CLIP:2 * REL_CLIP + 1]
        ext = jnp.where(lane > 2 * REL_CLIP, edge, rb)
        kc = lax.broadcasted_iota(jnp.int32, (nk, tq), 0) // CHUNK
        qc = lax.broadcasted_iota(jnp.int32, (nk, tq), 1) // CHUNK
        band = (kc >= qc) & (kc <= qc + C_LEFT_CHUNKS)
        for h in range(C_HEADS):
            rows = jnp.broadcast_to(ext[h:h + 1, :], (nk, ext.shape[1]))
            rolled = pltpu.roll(rows, REL_CLIP, axis=1, stride=1, stride_axis=0)
            bias_ref[h] = jnp.where(band, rolled[:, :tq] * LOG2E, NEG)

    u = u_ref[...]
    halo = jnp.where(j > 0, up_ref[...], 0.0)
    ext_u = jnp.concatenate([halo, u], axis=0)
    sums, w = {1: ext_u}, 1
    while w < max(POOL_WINDOWS):
        sums[2 * w] = sums[w] + pltpu.roll(sums[w], w, axis=0)
        w *= 2
    grp = lax.broadcasted_iota(jnp.int32, (1, B_WIDTH), 1) // B_GROUP_DIM
    wsum, win = sums[POOL_WINDOWS[-1]], POOL_WINDOWS[-1]
    for g in reversed(range(B_GROUPS - 1)):
        wsum = jnp.where(grp == g, sums[POOL_WINDOWS[g]], wsum)
        win = jnp.where(grp == g, POOL_WINDOWS[g], win)
    wsum = wsum[POOL_HALO:, :]
    pos = j * tq + lax.broadcasted_iota(jnp.int32, (tq, 1), 0)
    cnt = jnp.minimum(pos + 1, win).astype(F32)
    d = wsum / cnt - u
    yb = jnp.dot(d.astype(BF16), pw_ref[...], preferred_element_type=F32) * ps_ref[...]
    o_ref[:, 0:B_WIDTH] = yb.astype(o_ref.dtype)

    q = q_ref[...]
    kk = jnp.concatenate([k0_ref[...], k1_ref[...], k2_ref[...]], axis=0)
    vts = (v0_ref, v1_ref, v2_ref)
    head = lax.broadcasted_iota(jnp.int32, (1, C_WIDTH), 1) // HEAD_DIM
    for h in range(C_HEADS):
        qh = jnp.where(head == h, q, jnp.zeros_like(q))
        s_ref[h] = _dot_nt(kk, qh)
    rows = HEAD_DIM + SUM_ROWS
    krow = lax.broadcasted_iota(jnp.int32, (nk, tq), 0)
    kneg = jnp.where(krow >= (2 - j) * tq, 0.0, NEG)
    outs = []
    for h in range(C_HEADS):
        s = s_ref[h] + bias_ref[h] + kneg
        m = jnp.max(s, axis=0, keepdims=True)
        p = jnp.exp2(s - m).astype(BF16)
        pv = jnp.dot(vts[0][h * rows:(h + 1) * rows, :], p[0:tq, :], preferred_element_type=F32)
        for t in (1, 2):
            pv = pv + jnp.dot(vts[t][h * rows:(h + 1) * rows, :], p[t * tq:(t + 1) * tq, :],
                              preferred_element_type=F32)
        outs.append(pv[:HEAD_DIM, :] / pv[HEAD_DIM:HEAD_DIM + 1, :])
    out = jnp.concatenate(outs, axis=0).T
    o_ref[:, B_WIDTH:B_WIDTH + C_WIDTH] = out.astype(o_ref.dtype)


def _mix_bc(u, c, vct, rbp, pw, ps, layer, bsz, s_len):
    n = u.shape[0]
    tq = TQ_C
    assert 2 * tq == C_LEFT_CHUNKS * CHUNK and rbp.shape[-1] == 4 * tq
    nt = s_len // tq
    hpt = tq // POOL_HALO
    crows = vct.shape[1]
    cur = lambda col: (lambda b, j: (b * nt + j, col))
    prev = lambda back, col: (lambda b, j: (b * nt + jnp.maximum(j - back, 0), col))
    blk = lambda: (tq, C_WIDTH)
    vblk = lambda back: pl.BlockSpec((None, crows, tq),
                                     lambda b, j: (b * nt + jnp.maximum(j - back, 0), 0, 0))
    return pl.pallas_call(
        functools.partial(_mix_bc_kernel, tq=tq),
        grid=(bsz, nt),
        in_specs=[
            pl.BlockSpec((None,) + rbp.shape[1:], lambda b, j: (layer, 0, 0)),
            pl.BlockSpec((None,) + pw.shape[1:], lambda b, j: (layer, 0, 0)),
            pl.BlockSpec((None, 1, B_WIDTH), lambda b, j: (layer, 0, 0)),
            pl.BlockSpec((tq, B_WIDTH), cur(0)),
            pl.BlockSpec((POOL_HALO, B_WIDTH),
                         lambda b, j: ((b * nt + j) * hpt - jnp.minimum(j, 1), 0)),
            pl.BlockSpec(blk(), cur(0)),
            pl.BlockSpec(blk(), prev(2, 1)), pl.BlockSpec(blk(), prev(1, 1)), pl.BlockSpec(blk(), cur(1)),
            vblk(2), vblk(1), vblk(0),
        ],
        out_specs=pl.BlockSpec((tq, B_WIDTH + C_WIDTH), lambda b, j: (b * nt + j, 0)),
        out_shape=jax.ShapeDtypeStruct((n, B_WIDTH + C_WIDTH), BF16),
        scratch_shapes=[pltpu.VMEM((C_HEADS, 3 * tq, tq), F32),
                        pltpu.VMEM((C_HEADS, 3 * tq, tq), F32)],
        compiler_params=pltpu.CompilerParams(
            dimension_semantics=("arbitrary", "arbitrary"), vmem_limit_bytes=VMEM_LIMIT),
        name="mix_bc",
    )(rbp, pw, ps, u, u, c, c, c, c, vct, vct, vct)


def _post_kernel(x_ref, a_ref, bc_ref, wo_ref, g_ref, wup_ref, cw_ref, cb_ref, wd_ref, gf_ref,
                 o_ref, h_ref, acc_ref, carry_ref, a0_ref, g0_ref, a1_ref, g1_ref, hid_ref,
                 *, tiles_per_seq, final_norm):
    i = pl.program_id(0)
    tm = x_ref.shape[0]
    d_ff = wd_ref.shape[0]
    fc = a0_ref.shape[1]
    nch = d_ff // fc

    def cols(c, base=0):
        start = base + c * fc
        return pl.ds(start if isinstance(start, int) else pl.multiple_of(start, fc), fc)

    xm = (x_ref[...]
          + jnp.dot(a_ref[...], wo_ref[0:A_WIDTH, :], preferred_element_type=F32)
          + jnp.dot(bc_ref[...], wo_ref[A_WIDTH:, :], preferred_element_type=F32))
    h_ref[...] = _rms(xm, g_ref[...]).astype(BF16)
    acc_ref[...] = xm
    seq_start = (i % tiles_per_seq) == 0
    row = lax.broadcasted_iota(jnp.int32, (8, 1), 0)

    def up(c, a_ref, g_ref):
        h = h_ref[...]
        a_ref[...] = jnp.dot(h, wup_ref[:, cols(c)], preferred_element_type=F32)
        g_ref[...] = jnp.dot(h, wup_ref[:, cols(c, d_ff)], preferred_element_type=F32)

    def params(c):
        tail = jnp.where(seq_start, 0.0, carry_ref[c])
        return tail, cw_ref[:, cols(c)], cb_ref[:, cols(c)]

    def gate(c, a_ref, g_ref, prm):
        tail, cw, cb = prm
        a = a_ref[...]
        carry_ref[c] = a[tm - 8:, :]
        r1 = pltpu.roll(a, 1, axis=0)
        r2 = pltpu.roll(a, 2, axis=0)
        top1 = jnp.where(row == 0, tail[7:8, :], r1[:8, :])
        top2 = jnp.where(row == 0, tail[6:7, :], jnp.where(row == 1, tail[7:8, :], r2[:8, :]))
        a1 = jnp.concatenate([top1, r1[8:, :]], axis=0)
        a2 = jnp.concatenate([top2, r2[8:, :]], axis=0)
        conv = cb + cw[0:1, :] * a2
        conv = conv + cw[1:2, :] * a1
        conv = conv + cw[2:3, :] * a
        hid_ref[:, cols(c)] = (conv * jax.nn.sigmoid(conv) * g_ref[...]).astype(BF16)

    assert nch % 2 == 1 and nch >= 3
    up(0, a0_ref, g0_ref)
    up(1, a1_ref, g1_ref)

    def pair(pp, carry):
        c = 2 * pp
        prm0, prm1 = params(c), params(c + 1)
        gate(c, a0_ref, g0_ref, prm0)
        up(c + 2, a0_ref, g0_ref)
        gate(c + 1, a1_ref, g1_ref, prm1)
        up(c + 3, a1_ref, g1_ref)
        return carry

    npair = (nch - 3) // 2
    assert npair % 2 == 0
    lax.fori_loop(0, npair // 2, lambda qq, carry: pair(2 * qq + 1, pair(2 * qq, carry)), 0)
    gate(nch - 3, a0_ref, g0_ref, params(nch - 3))
    up(nch - 1, a0_ref, g0_ref)
    gate(nch - 2, a1_ref, g1_ref, params(nch - 2))
    gate(nch - 1, a0_ref, g0_ref, params(nch - 1))
    out = acc_ref[...] + jnp.dot(hid_ref[...], wd_ref[...], preferred_element_type=F32)
    if final_norm:
        out = _rms(out, gf_ref[...])
    o_ref[...] = out


def _post(x2, oa, obc, wo, ffn_norm, wup, cw, cb, wd, gf, layer, s_len, final_norm):
    n, d = x2.shape
    tm, fc = TM_POST, FF_CHUNK
    d_ff = wd.shape[1]
    nch = d_ff // fc
    const = pl.Buffered(1)
    l3 = lambda i: (layer, 0, 0)
    return pl.pallas_call(
        functools.partial(_post_kernel, tiles_per_seq=s_len // tm, final_norm=final_norm),
        grid=(n // tm,),
        in_specs=[
            pl.BlockSpec((tm, d), lambda i: (i, 0)),
            pl.BlockSpec((tm, A_WIDTH), lambda i: (i, 0)),
            pl.BlockSpec((tm, B_WIDTH + C_WIDTH), lambda i: (i, 0)),
            pl.BlockSpec((None,) + wo.shape[1:], l3, pipeline_mode=const),
            pl.BlockSpec((None, 1, d), l3),
            pl.BlockSpec((None,) + wup.shape[1:], l3, pipeline_mode=const),
            pl.BlockSpec((None,) + cw.shape[1:], l3),
            pl.BlockSpec((None,) + cb.shape[1:], l3),
            pl.BlockSpec((None,) + wd.shape[1:], l3, pipeline_mode=const),
            pl.BlockSpec((1, d), lambda i: (0, 0)),
        ],
        out_specs=pl.BlockSpec((tm, d), lambda i: (i, 0)),
        out_shape=jax.ShapeDtypeStruct((n, d), F32),
        scratch_shapes=[
            pltpu.VMEM((tm, d), BF16),
            pltpu.VMEM((tm, d), F32),
            pltpu.VMEM((nch, 8, fc), F32),
            pltpu.VMEM((tm, fc), F32), pltpu.VMEM((tm, fc), F32),
            pltpu.VMEM((tm, fc), F32), pltpu.VMEM((tm, fc), F32),
            pltpu.VMEM((tm, d_ff), BF16),
        ],
        compiler_params=pltpu.CompilerParams(
            dimension_semantics=("arbitrary",), vmem_limit_bytes=VMEM_LIMIT),
        name="post",
    )(x2, oa, obc, wo, ffn_norm, wup, cw, cb, wd, gf)


def kernel(x, attn_norm, w_in, lam_q1, lam_k1, lam_q2, lam_k2, diff_subln, pool_w, pool_scale,
           rel_bias, w_out, ffn_norm, w_up, conv_w, conv_b, w_down, final_norm):
    bsz, s_len, d = x.shape
    depth = w_in.shape[0]
    d_ff = w_down.shape[1]
    assert d_ff % FF_CHUNK == 0 and s_len % TQ_A == 0 and s_len % TM_POST == 0
    assert all(w & (w - 1) == 0 and w <= POOL_HALO for w in POOL_WINDOWS) and len(POOL_WINDOWS) == B_GROUPS

    w_in_b = w_in.astype(BF16)
    wo = w_out.astype(BF16)
    wup = w_up.astype(BF16)
    wd = w_down.astype(BF16)
    cw = conv_w
    cb = conv_b[:, None, :]
    lamvec = jnp.stack([lam_q1, lam_k1, lam_q2, lam_k2], axis=1)
    gain = diff_subln[:, :, None]
    eye = jnp.eye(B_GROUPS, dtype=pool_w.dtype)
    pw = jnp.einsum('lgcd,gh->lgchd', pool_w, eye).reshape(depth, B_WIDTH, B_WIDTH).astype(BF16)
    ps = pool_scale[:, None, :]
    rbp = jnp.pad(rel_bias, ((0, 0), (0, 0), (0, 4 * TQ_C - rel_bias.shape[-1])))
    an = attn_norm[:, None, :]
    fn = ffn_norm[:, None, :]
    gf = final_norm[None, :]

    x2 = x.reshape(bsz * s_len, d)
    for layer in range(depth):
        qk, vt, u, c, vct = _in_proj(x2, an, w_in_b, layer)
        oa = _attn_a(qk, vt, lamvec, gain, layer, bsz, s_len)
        obc = _mix_bc(u, c, vct, rbp, pw, ps, layer, bsz, s_len)
        x2 = _post(x2, oa, obc, wo, fn, wup, cw, cb, wd, gf, layer, s_len,
                   final_norm=(layer == depth - 1))
    return x2.reshape(bsz, s_len, d)
```

```python
import functools
import math

import jax
import jax.numpy as jnp
from jax import lax
from jax.experimental import pallas as pl
from jax.experimental.pallas import tpu as pltpu

F32 = jnp.float32
BF16 = jnp.bfloat16

CHUNK = 64
HEAD_DIM = 64
A_HEADS = 4
A_V = 2 * HEAD_DIM
A_WIDTH = A_HEADS * A_V
B_GROUPS = 4
B_GROUP_DIM = 64
B_WIDTH = B_GROUPS * B_GROUP_DIM
POOL_WINDOWS = (2, 4, 8, 16)
C_HEADS = 4
C_WIDTH = C_HEADS * HEAD_DIM
C_LEFT_CHUNKS = 8
REL_CLIP = 256
EPS = 1e-5
NEG = -1e30
LOG2E = math.log2(math.e)
SUM_ROWS = 16

TM_PROJ = 512
TQ_A = 2048
TK_A = 512
STRIP_A = 512
DIAG_SUB = 256
TQ_C = 256
TM_POST = 512
FF_CHUNK = 256
POOL_HALO = 16
VMEM_LIMIT = 56 * 1024 * 1024


def _rms(x, g):
    ms = jnp.mean(x * x, axis=-1, keepdims=True)
    return x * lax.rsqrt(ms + EPS) * g


def _dot_nt(a, b):
    return lax.dot_general(a, b, (((1,), (1,)), ((), ())), preferred_element_type=F32)


def _in_proj_kernel(x_ref, g_ref, w_ref, qk_ref, vt_ref, u_ref, c_ref, vct_ref, wvt_ref, wvct_ref,
                    *, tk, tc):
    v0 = 2 * A_WIDTH
    u0 = 3 * A_WIDTH
    c0 = u0 + B_WIDTH
    vc0 = c0 + 2 * C_WIDTH

    @pl.when(pl.program_id(0) == 0)
    def _transpose_value_weights():
        wvt_ref[...] = w_ref[:, v0:u0].astype(F32).T.astype(BF16)
        wvct_ref[...] = w_ref[:, vc0:].astype(F32).T.astype(BF16)

    h = _rms(x_ref[...], g_ref[...]).astype(BF16)
    qk = jnp.dot(h, w_ref[:, :v0], preferred_element_type=F32)
    col = lax.broadcasted_iota(jnp.int32, (1, qk.shape[1]), 1)
    qscale = HEAD_DIM ** -0.5 * LOG2E
    qk_ref[...] = (qk * jnp.where(col < A_WIDTH, qscale, 1.0)).astype(BF16)
    vt = _dot_nt(wvt_ref[...], h).astype(BF16)
    rows = A_V + SUM_ROWS
    for t in range(vt_ref.shape[0]):
        for hd in range(A_HEADS):
            vt_ref[t, hd * rows:hd * rows + A_V, :] = vt[hd * A_V:(hd + 1) * A_V, t * tk:(t + 1) * tk]
            vt_ref[t, hd * rows + A_V:(hd + 1) * rows, :] = jnp.ones((SUM_ROWS, tk), BF16)
    u_ref[...] = jnp.dot(h, w_ref[:, u0:c0], preferred_element_type=F32)
    c = jnp.dot(h, w_ref[:, c0:vc0], preferred_element_type=F32)
    colc = lax.broadcasted_iota(jnp.int32, (1, c.shape[1]), 1)
    c_ref[...] = (c * jnp.where(colc < C_WIDTH, qscale, 1.0)).astype(BF16)
    vct = _dot_nt(wvct_ref[...], h).astype(BF16)
    rows = HEAD_DIM + SUM_ROWS
    for t in range(vct_ref.shape[0]):
        for hd in range(C_HEADS):
            vct_ref[t, hd * rows:hd * rows + HEAD_DIM, :] = (
                vct[hd * HEAD_DIM:(hd + 1) * HEAD_DIM, t * tc:(t + 1) * tc])
            vct_ref[t, hd * rows + HEAD_DIM:(hd + 1) * rows, :] = jnp.ones((SUM_ROWS, tc), BF16)


def _in_proj(x2, attn_norm, w_in_b, layer):
    n, d = x2.shape
    tm, tk, tc = TM_PROJ, TK_A, TQ_C
    lsel = lambda *_: (layer, 0, 0)
    arows = A_HEADS * (A_V + SUM_ROWS)
    crows = C_HEADS * (HEAD_DIM + SUM_ROWS)
    return pl.pallas_call(
        functools.partial(_in_proj_kernel, tk=tk, tc=tc),
        grid=(n // tm,),
        in_specs=[
            pl.BlockSpec((tm, d), lambda i: (i, 0)),
            pl.BlockSpec((None, 1, d), lsel),
            pl.BlockSpec((None,) + w_in_b.shape[1:], lsel),
        ],
        out_specs=[
            pl.BlockSpec((tm, 2 * A_WIDTH), lambda i: (i, 0)),
            pl.BlockSpec((tm // tk, arows, tk), lambda i: (i, 0, 0)),
            pl.BlockSpec((tm, B_WIDTH), lambda i: (i, 0)),
            pl.BlockSpec((tm, 2 * C_WIDTH), lambda i: (i, 0)),
            pl.BlockSpec((tm // tc, crows, tc), lambda i: (i, 0, 0)),
        ],
        out_shape=[
            jax.ShapeDtypeStruct((n, 2 * A_WIDTH), BF16),
            jax.ShapeDtypeStruct((n // tk, arows, tk), BF16),
            jax.ShapeDtypeStruct((n, B_WIDTH), F32),
            jax.ShapeDtypeStruct((n, 2 * C_WIDTH), BF16),
            jax.ShapeDtypeStruct((n // tc, crows, tc), BF16),
        ],
        scratch_shapes=[
            pltpu.VMEM((A_WIDTH, d), BF16),
            pltpu.VMEM((C_WIDTH, d), BF16),
        ],
        compiler_params=pltpu.CompilerParams(
            dimension_semantics=("arbitrary",), vmem_limit_bytes=VMEM_LIMIT),
        name="in_proj",
    )(x2, attn_norm, w_in_b)


def _attn_a_kernel(lam_ref, gain_ref, q_ref, k_ref, vt_ref, o_ref,
                   qcat_ref, sa_ref, sb_ref, m_ref, l_ref, acc_ref,
                   *, tq, tk, strip, nq, lam_init):
    nstrip = 2 * tq // strip

    def load_queries(i):
        q = q_ref[pl.ds(pl.multiple_of(i * tq, tq), tq), :]
        lane = lax.broadcasted_iota(jnp.int32, q.shape, 1)
        zero = jnp.zeros_like(q)
        qcat_ref[0:tq, :] = jnp.where(lane < HEAD_DIM, q, zero)
        qcat_ref[tq:2 * tq, :] = jnp.where(lane >= HEAD_DIM, q, zero)

    def reset_state():
        m_ref[...] = jnp.full(m_ref.shape, NEG, F32)
        l_ref[...] = jnp.zeros(l_ref.shape, F32)
        acc_ref[...] = jnp.zeros(acc_ref.shape, F32)

    def visible(c, diag):
        qblock = ((c * strip) % tq) // tk
        return 2 if diag is None or qblock > diag else int(qblock == diag)

    def pieces(c, diag):
        kind = visible(c, diag)
        if kind == 0:
            return []
        if kind == 2:
            return [(c * strip, strip, tk, False)]
        out = []
        for h in range(strip // DIAG_SUB):
            off = (c * strip) % tk + h * DIAG_SUB
            out.append((c * strip + h * DIAG_SUB, DIAG_SUB, off + DIAG_SUB, True))
        return out

    def scores(j, c, dst_ref, diag=None):
        for col, width, keys, _ in pieces(c, diag):
            k_t = k_ref[pl.ds(pl.multiple_of(j * tk, tk), keys), :]
            dst_ref[0:keys, col:col + width] = _dot_nt(k_t, qcat_ref[col:col + width, :])

    def consume(t, c, src_ref, diag=None):
        for col, width, keys, masked in pieces(c, diag):
            cs = slice(col, col + width)
            s = src_ref[0:keys, cs]
            if masked:
                kc = lax.broadcasted_iota(jnp.int32, s.shape, 0) // CHUNK
                qc = (col % tk + lax.broadcasted_iota(jnp.int32, s.shape, 1)) // CHUNK
                s = jnp.where(kc <= qc, s, NEG)
            m_old = m_ref[:, cs]
            m_new = jnp.maximum(m_old, jnp.max(s, axis=0, keepdims=True))
            alpha = jnp.exp2(m_old - m_new)
            p = jnp.exp2(s - m_new).astype(BF16)
            pv = jnp.dot(vt_ref[t, :, 0:keys], p, preferred_element_type=F32)
            l_ref[:, cs] = alpha * l_ref[:, cs] + pv[A_V:A_V + 1, :]
            acc_ref[:, cs] = alpha * acc_ref[:, cs] + pv[:A_V, :]
            m_ref[:, cs] = m_new

    def pipelined(j, dst_ref, src_ref, diag_next=None, diag=None):
        for c in range(nstrip):
            scores(j, c, dst_ref, diag_next)
            consume(j - 1, c, src_ref, diag)

    def finalize(i):
        lv = lam_ref[...]
        lam = (jnp.exp(jnp.sum(lv[0:1] * lv[1:2], axis=1, keepdims=True))
               - jnp.exp(jnp.sum(lv[2:3] * lv[3:4], axis=1, keepdims=True)) + lam_init)
        inv = 1.0 / l_ref[...]
        a = (acc_ref[:, :tq] * inv[:, :tq]
             - acc_ref[:, tq:] * (lam * inv[:, tq:]))
        ms = jnp.mean(a * a, axis=0, keepdims=True)
        y = a * lax.rsqrt(ms + EPS) * (gain_ref[...] * (1.0 - lam_init))
        o_ref[pl.ds(pl.multiple_of(i * tq, tq), tq), :] = y.T.astype(o_ref.dtype)

    def pair(mm, carry):
        pipelined(2 * mm - 1, sb_ref, sa_ref)
        pipelined(2 * mm, sa_ref, sb_ref)
        return carry

    ratio = tq // tk
    assert ratio % 2 == 0
    load_queries(0)
    reset_state()
    for c in range(nstrip):
        scores(0, c, sa_ref)

    def query_tile(i, carry):
        first = ratio * i
        lax.fori_loop(1, first // 2 + 1, pair, 0)
        bufs = (sa_ref, sb_ref)
        for r in range(ratio - 1):
            pipelined(first + r + 1, bufs[(r + 1) % 2], bufs[r % 2], diag_next=r + 1, diag=r)
        load_queries(jnp.minimum(i + 1, nq - 1))
        for c in range(nstrip):
            scores(0, c, sa_ref)
            consume(first + ratio - 1, c, sb_ref, diag=ratio - 1)
        finalize(i)
        reset_state()
        return carry

    lax.fori_loop(0, nq, query_tile, 0)


def _attn_a(qk, vt, lamvec, gain, layer, bsz, s_len):
    n = qk.shape[0]
    tq, tk = TQ_A, TK_A
    assert tq % (2 * tk) == 0 and tk % STRIP_A == 0
    nq = s_len // tq
    lam_init = 0.8 - 0.6 * math.exp(-0.3 * layer)
    return pl.pallas_call(
        functools.partial(_attn_a_kernel, tq=tq, tk=tk, strip=STRIP_A, nq=nq, lam_init=lam_init),
        grid=(bsz, A_HEADS),
        in_specs=[
            pl.BlockSpec((None, 4, HEAD_DIM), lambda b, h: (layer, 0, 0)),
            pl.BlockSpec((None, A_V, 1), lambda b, h: (layer, 0, 0)),
            pl.BlockSpec((s_len, A_V), lambda b, h: (b, h)),
            pl.BlockSpec((s_len, A_V), lambda b, h: (b, A_HEADS + h)),
            pl.BlockSpec((s_len // tk, A_V + SUM_ROWS, tk), lambda b, h: (b, h, 0)),
        ],
        out_specs=pl.BlockSpec((s_len, A_V), lambda b, h: (b, h)),
        out_shape=jax.ShapeDtypeStruct((n, A_WIDTH), BF16),
        scratch_shapes=[
            pltpu.VMEM((2 * tq, A_V), BF16),
            pltpu.VMEM((tk, 2 * tq), F32),
            pltpu.VMEM((tk, 2 * tq), F32),
            pltpu.VMEM((1, 2 * tq), F32),
            pltpu.VMEM((1, 2 * tq), F32),
            pltpu.VMEM((A_V, 2 * tq), F32),
        ],
        compiler_params=pltpu.CompilerParams(
            dimension_semantics=("arbitrary", "arbitrary"),
            vmem_limit_bytes=VMEM_LIMIT),
        name="attn_a",
    )(lamvec, gain, qk, qk, vt)


def _mix_bc_kernel(rb_ref, pw_ref, ps_ref, u_ref, up_ref, q_ref,
                   k0_ref, k1_ref, k2_ref, v0_ref, v1_ref, v2_ref,
                   o_ref, bias_ref, s_ref, *, tq):
    j = pl.program_id(1)
    nk = 3 * tq

    @pl.when((pl.program_id(0) == 0) & (j == 0))
    def _build_bias():
        rb = rb_ref[...]
        lane = lax.broadcasted_iota(jnp.int32, rb.shape, 1)
        edge = rb[:, 2 * REL_CLIP:2 * REL_CLIP + 1]
        ext = jnp.where(lane > 2 * REL_CLIP, edge, rb)
        kc = lax.broadcasted_iota(jnp.int32, (nk, tq), 0) // CHUNK
        qc = lax.broadcasted_iota(jnp.int32, (nk, tq), 1) // CHUNK
        band = (kc >= qc) & (kc <= qc + C_LEFT_CHUNKS)
        for h in range(C_HEADS):
            rows = jnp.broadcast_to(ext[h:h + 1, :], (nk, ext.shape[1]))
            rolled = pltpu.roll(rows, REL_CLIP, axis=1, stride=1, stride_axis=0)
            bias_ref[h] = jnp.where(band, rolled[:, :tq] * LOG2E, NEG)

    u = u_ref[...]
    halo = jnp.where(j > 0, up_ref[...], 0.0)
    ext_u = jnp.concatenate([halo, u], axis=0)
    sums, w = {1: ext_u}, 1
    while w < max(POOL_WINDOWS):
        sums[2 * w] = sums[w] + pltpu.roll(sums[w], w, axis=0)
        w *= 2
    grp = lax.broadcasted_iota(jnp.int32, (1, B_WIDTH), 1) // B_GROUP_DIM
    wsum, win = sums[POOL_WINDOWS[-1]], POOL_WINDOWS[-1]
    for g in reversed(range(B_GROUPS - 1)):
        wsum = jnp.where(grp == g, sums[POOL_WINDOWS[g]], wsum)
        win = jnp.where(grp == g, POOL_WINDOWS[g], win)
    wsum = wsum[POOL_HALO:, :]
    pos = j * tq + lax.broadcasted_iota(jnp.int32, (tq, 1), 0)
    cnt = jnp.minimum(pos + 1, win).astype(F32)
    d = wsum / cnt - u
    yb = jnp.dot(d.astype(BF16), pw_ref[...], preferred_element_type=F32) * ps_ref[...]
    o_ref[:, 0:B_WIDTH] = yb.astype(o_ref.dtype)

    q = q_ref[...]
    kk = jnp.concatenate([k0_ref[...], k1_ref[...], k2_ref[...]], axis=0)
    vts = (v0_ref, v1_ref, v2_ref)
    head = lax.broadcasted_iota(jnp.int32, (1, C_WIDTH), 1) // HEAD_DIM
    for h in range(C_HEADS):
        qh = jnp.where(head == h, q, jnp.zeros_like(q))
        s_ref[h] = _dot_nt(kk, qh)
    rows = HEAD_DIM + SUM_ROWS
    krow = lax.broadcasted_iota(jnp.int32, (nk, tq), 0)
    kneg = jnp.where(krow >= (2 - j) * tq, 0.0, NEG)
    outs = []
    for h in range(C_HEADS):
        s = s_ref[h] + bias_ref[h] + kneg
        m = jnp.max(s, axis=0, keepdims=True)
        p = jnp.exp2(s - m).astype(BF16)
        pv = jnp.dot(vts[0][h * rows:(h + 1) * rows, :], p[0:tq, :], preferred_element_type=F32)
        for t in (1, 2):
            pv = pv + jnp.dot(vts[t][h * rows:(h + 1) * rows, :], p[t * tq:(t + 1) * tq, :],
                              preferred_element_type=F32)
        outs.append(pv[:HEAD_DIM, :] / pv[HEAD_DIM:HEAD_DIM + 1, :])
    out = jnp.concatenate(outs, axis=0).T
    o_ref[:, B_WIDTH:B_WIDTH + C_WIDTH] = out.astype(o_ref.dtype)


def _mix_bc(u, c, vct, rbp, pw, ps, layer, bsz, s_len):
    n = u.shape[0]
    tq = TQ_C
    assert 2 * tq == C_LEFT_CHUNKS * CHUNK and rbp.shape[-1] == 4 * tq
    nt = s_len // tq
    hpt = tq // POOL_HALO
    crows = vct.shape[1]
    cur = lambda col: (lambda b, j: (b * nt + j, col))
    prev = lambda back, col: (lambda b, j: (b * nt + jnp.maximum(j - back, 0), col))
    blk = lambda: (tq, C_WIDTH)
    vblk = lambda back: pl.BlockSpec((None, crows, tq),
                                     lambda b, j: (b * nt + jnp.maximum(j - back, 0), 0, 0))
    return pl.pallas_call(
        functools.partial(_mix_bc_kernel, tq=tq),
        grid=(bsz, nt),
        in_specs=[
            pl.BlockSpec((None,) + rbp.shape[1:], lambda b, j: (layer, 0, 0)),
            pl.BlockSpec((None,) + pw.shape[1:], lambda b, j: (layer, 0, 0)),
            pl.BlockSpec((None, 1, B_WIDTH), lambda b, j: (layer, 0, 0)),
            pl.BlockSpec((tq, B_WIDTH), cur(0)),
            pl.BlockSpec((POOL_HALO, B_WIDTH),
                         lambda b, j: ((b * nt + j) * hpt - jnp.minimum(j, 1), 0)),
            pl.BlockSpec(blk(), cur(0)),
            pl.BlockSpec(blk(), prev(2, 1)), pl.BlockSpec(blk(), prev(1, 1)), pl.BlockSpec(blk(), cur(1)),
            vblk(2), vblk(1), vblk(0),
        ],
        out_specs=pl.BlockSpec((tq, B_WIDTH + C_WIDTH), lambda b, j: (b * nt + j, 0)),
        out_shape=jax.ShapeDtypeStruct((n, B_WIDTH + C_WIDTH), BF16),
        scratch_shapes=[pltpu.VMEM((C_HEADS, 3 * tq, tq), F32),
                        pltpu.VMEM((C_HEADS, 3 * tq, tq), F32)],
        compiler_params=pltpu.CompilerParams(
            dimension_semantics=("arbitrary", "arbitrary"), vmem_limit_bytes=VMEM_LIMIT),
        name="mix_bc",
    )(rbp, pw, ps, u, u, c, c, c, c, vct, vct, vct)


def _post_kernel(x_ref, a_ref, bc_ref, wo_ref, g_ref, wup_ref, cw_ref, cb_ref, wd_ref, gf_ref,
                 o_ref, h_ref, acc_ref, carry_ref, a0_ref, g0_ref, a1_ref, g1_ref, hid_ref,
                 *, tiles_per_seq, final_norm):
    i = pl.program_id(0)
    tm = x_ref.shape[0]
    d_ff = wd_ref.shape[0]
    fc = a0_ref.shape[1]
    nch = d_ff // fc

    def cols(c, base=0):
        start = base + c * fc
        return pl.ds(start if isinstance(start, int) else pl.multiple_of(start, fc), fc)

    xm = (x_ref[...]
          + jnp.dot(a_ref[...], wo_ref[0:A_WIDTH, :], preferred_element_type=F32)
          + jnp.dot(bc_ref[...], wo_ref[A_WIDTH:, :], preferred_element_type=F32))
    h_ref[...] = _rms(xm, g_ref[...]).astype(BF16)
    acc_ref[...] = xm
    seq_start = (i % tiles_per_seq) == 0
    row = lax.broadcasted_iota(jnp.int32, (8, 1), 0)

    def up(c, a_ref, g_ref):
        h = h_ref[...]
        a_ref[...] = jnp.dot(h, wup_ref[:, cols(c)], preferred_element_type=F32)
        g_ref[...] = jnp.dot(h, wup_ref[:, cols(c, d_ff)], preferred_element_type=F32)

    def params(c):
        tail = jnp.where(seq_start, 0.0, carry_ref[c])
        return tail, cw_ref[:, cols(c)], cb_ref[:, cols(c)]

    def gate(c, a_ref, g_ref, prm):
        tail, cw, cb = prm
        a = a_ref[...]
        carry_ref[c] = a[tm - 8:, :]
        r1 = pltpu.roll(a, 1, axis=0)
        r2 = pltpu.roll(a, 2, axis=0)
        top1 = jnp.where(row == 0, tail[7:8, :], r1[:8, :])
        top2 = jnp.where(row == 0, tail[6:7, :], jnp.where(row == 1, tail[7:8, :], r2[:8, :]))
        a1 = jnp.concatenate([top1, r1[8:, :]], axis=0)
        a2 = jnp.concatenate([top2, r2[8:, :]], axis=0)
        conv = cb + cw[0:1, :] * a2
        conv = conv + cw[1:2, :] * a1
        conv = conv + cw[2:3, :] * a
        hid_ref[:, cols(c)] = (conv * jax.nn.sigmoid(conv) * g_ref[...]).astype(BF16)

    assert nch % 2 == 1 and nch >= 3
    up(0, a0_ref, g0_ref)
    up(1, a1_ref, g1_ref)

    def pair(pp, carry):
        c = 2 * pp
        prm0, prm1 = params(c), params(c + 1)
        gate(c, a0_ref, g0_ref, prm0)
        up(c + 2, a0_ref, g0_ref)
        gate(c + 1, a1_ref, g1_ref, prm1)
        up(c + 3, a1_ref, g1_ref)
        return carry

    npair = (nch - 3) // 2
    assert npair % 2 == 0
    for pp in range(npair):
        pair(pp, 0)
    gate(nch - 3, a0_ref, g0_ref, params(nch - 3))
    up(nch - 1, a0_ref, g0_ref)
    gate(nch - 2, a1_ref, g1_ref, params(nch - 2))
    gate(nch - 1, a0_ref, g0_ref, params(nch - 1))
    out = acc_ref[...] + jnp.dot(hid_ref[...], wd_ref[...], preferred_element_type=F32)
    if final_norm:
        out = _rms(out, gf_ref[...])
    o_ref[...] = out


def _post(x2, oa, obc, wo, ffn_norm, wup, cw, cb, wd, gf, layer, s_len, final_norm):
    n, d = x2.shape
    tm, fc = TM_POST, FF_CHUNK
    d_ff = wd.shape[1]
    nch = d_ff // fc
    const = pl.Buffered(1)
    l3 = lambda i: (layer, 0, 0)
    return pl.pallas_call(
        functools.partial(_post_kernel, tiles_per_seq=s_len // tm, final_norm=final_norm),
        grid=(n // tm,),
        in_specs=[
            pl.BlockSpec((tm, d), lambda i: (i, 0)),
            pl.BlockSpec((tm, A_WIDTH), lambda i: (i, 0)),
            pl.BlockSpec((tm, B_WIDTH + C_WIDTH), lambda i: (i, 0)),
            pl.BlockSpec((None,) + wo.shape[1:], l3, pipeline_mode=const),
            pl.BlockSpec((None, 1, d), l3),
            pl.BlockSpec((None,) + wup.shape[1:], l3, pipeline_mode=const),
            pl.BlockSpec((None,) + cw.shape[1:], l3),
            pl.BlockSpec((None,) + cb.shape[1:], l3),
            pl.BlockSpec((None,) + wd.shape[1:], l3, pipeline_mode=const),
            pl.BlockSpec((1, d), lambda i: (0, 0)),
        ],
        out_specs=pl.BlockSpec((tm, d), lambda i: (i, 0)),
        out_shape=jax.ShapeDtypeStruct((n, d), F32),
        scratch_shapes=[
            pltpu.VMEM((tm, d), BF16),
            pltpu.VMEM((tm, d), F32),
            pltpu.VMEM((nch, 8, fc), F32),
            pltpu.VMEM((tm, fc), F32), pltpu.VMEM((tm, fc), F32),
            pltpu.VMEM((tm, fc), F32), pltpu.VMEM((tm, fc), F32),
            pltpu.VMEM((tm, d_ff), BF16),
        ],
        compiler_params=pltpu.CompilerParams(
            dimension_semantics=("arbitrary",), vmem_limit_bytes=VMEM_LIMIT),
        name="post",
    )(x2, oa, obc, wo, ffn_norm, wup, cw, cb, wd, gf)


def kernel(x, attn_norm, w_in, lam_q1, lam_k1, lam_q2, lam_k2, diff_subln, pool_w, pool_scale,
           rel_bias, w_out, ffn_norm, w_up, conv_w, conv_b, w_down, final_norm):
    bsz, s_len, d = x.shape
    depth = w_in.shape[0]
    d_ff = w_down.shape[1]
    assert d_ff % FF_CHUNK == 0 and s_len % TQ_A == 0 and s_len % TM_POST == 0
    assert all(w & (w - 1) == 0 and w <= POOL_HALO for w in POOL_WINDOWS) and len(POOL_WINDOWS) == B_GROUPS

    w_in_b = w_in.astype(BF16)
    wo = w_out.astype(BF16)
    wup = w_up.astype(BF16)
    wd = w_down.astype(BF16)
    cw = conv_w
    cb = conv_b[:, None, :]
    lamvec = jnp.stack([lam_q1, lam_k1, lam_q2, lam_k2], axis=1)
    gain = diff_subln[:, :, None]
    eye = jnp.eye(B_GROUPS, dtype=pool_w.dtype)
    pw = jnp.einsum('lgcd,gh->lgchd', pool_w, eye).reshape(depth, B_WIDTH, B_WIDTH).astype(BF16)
    ps = pool_scale[:, None, :]
    rbp = jnp.pad(rel_bias, ((0, 0), (0, 0), (0, 4 * TQ_C - rel_bias.shape[-1])))
    an = attn_norm[:, None, :]
    fn = ffn_norm[:, None, :]
    gf = final_norm[None, :]

    x2 = x.reshape(bsz * s_len, d)
    for layer in range(depth):
        qk, vt, u, c, vct = _in_proj(x2, an, w_in_b, layer)
        oa = _attn_a(qk, vt, lamvec, gain, layer, bsz, s_len)
        obc = _mix_bc(u, c, vct, rbp, pw, ps, layer, bsz, s_len)
        x2 = _post(x2, oa, obc, wo, fn, wup, cw, cb, wd, gf, layer, s_len,
                   final_norm=(layer == depth - 1))
    return x2.reshape(bsz, s_len, d)
```

```python
import functools
import math

import jax
import jax.numpy as jnp
from jax import lax
from jax.experimental import pallas as pl
from jax.experimental.pallas import tpu as pltpu

F32 = jnp.float32
BF16 = jnp.bfloat16

CHUNK = 64
HEAD_DIM = 64
A_HEADS = 4
A_V = 2 * HEAD_DIM
A_WIDTH = A_HEADS * A_V
B_GROUPS = 4
B_GROUP_DIM = 64
B_WIDTH = B_GROUPS * B_GROUP_DIM
POOL_WINDOWS = (2, 4, 8, 16)
C_HEADS = 4
C_WIDTH = C_HEADS * HEAD_DIM
C_LEFT_CHUNKS = 8
REL_CLIP = 256
EPS = 1e-5
NEG = -1e30
LOG2E = math.log2(math.e)
SUM_ROWS = 16

TM_PROJ = 512
TQ_A = 2048
TK_A = 512
STRIP_A = 512
DIAG_SUB = 256
TQ_C = 256
TM_POST = 512
FF_CHUNK = 256
POOL_HALO = 16
VMEM_LIMIT = 56 * 1024 * 1024


def _rms(x, g):
    ms = jnp.mean(x * x, axis=-1, keepdims=True)
    return x * lax.rsqrt(ms + EPS) * g


def _dot_nt(a, b):
    return lax.dot_general(a, b, (((1,), (1,)), ((), ())), preferred_element_type=F32)


def _in_proj_kernel(x_ref, g_ref, w_ref, qk_ref, vt_ref, u_ref, c_ref, vct_ref, wvt_ref, wvct_ref,
                    *, tk, tc):
    v0 = 2 * A_WIDTH
    u0 = 3 * A_WIDTH
    c0 = u0 + B_WIDTH
    vc0 = c0 + 2 * C_WIDTH

    @pl.when(pl.program_id(0) == 0)
    def _transpose_value_weights():
        wvt_ref[...] = w_ref[:, v0:u0].astype(F32).T.astype(BF16)
        wvct_ref[...] = w_ref[:, vc0:].astype(F32).T.astype(BF16)

    h = _rms(x_ref[...], g_ref[...]).astype(BF16)
    qk = jnp.dot(h, w_ref[:, :v0], preferred_element_type=F32)
    col = lax.broadcasted_iota(jnp.int32, (1, qk.shape[1]), 1)
    qscale = HEAD_DIM ** -0.5 * LOG2E
    qk_ref[...] = (qk * jnp.where(col < A_WIDTH, qscale, 1.0)).astype(BF16)
    vt = _dot_nt(wvt_ref[...], h).astype(BF16)
    rows = A_V + SUM_ROWS
    for t in range(vt_ref.shape[0]):
        for hd in range(A_HEADS):
            vt_ref[t, hd * rows:hd * rows + A_V, :] = vt[hd * A_V:(hd + 1) * A_V, t * tk:(t + 1) * tk]
            vt_ref[t, hd * rows + A_V:(hd + 1) * rows, :] = jnp.ones((SUM_ROWS, tk), BF16)
    u_ref[...] = jnp.dot(h, w_ref[:, u0:c0], preferred_element_type=F32)
    c = jnp.dot(h, w_ref[:, c0:vc0], preferred_element_type=F32)
    colc = lax.broadcasted_iota(jnp.int32, (1, c.shape[1]), 1)
    c_ref[...] = (c * jnp.where(colc < C_WIDTH, qscale, 1.0)).astype(BF16)
    vct = _dot_nt(wvct_ref[...], h).astype(BF16)
    rows = HEAD_DIM + SUM_ROWS
    for t in range(vct_ref.shape[0]):
        for hd in range(C_HEADS):
            vct_ref[t, hd * rows:hd * rows + HEAD_DIM, :] = (
                vct[hd * HEAD_DIM:(hd + 1) * HEAD_DIM, t * tc:(t + 1) * tc])
            vct_ref[t, hd * rows + HEAD_DIM:(hd + 1) * rows, :] = jnp.ones((SUM_ROWS, tc), BF16)


def _in_proj(x2, attn_norm, w_in_b, layer):
    n, d = x2.shape
    tm, tk, tc = TM_PROJ, TK_A, TQ_C
    lsel = lambda *_: (layer, 0, 0)
    arows = A_HEADS * (A_V + SUM_ROWS)
    crows = C_HEADS * (HEAD_DIM + SUM_ROWS)
    return pl.pallas_call(
        functools.partial(_in_proj_kernel, tk=tk, tc=tc),
        grid=(n // tm,),
        in_specs=[
            pl.BlockSpec((tm, d), lambda i: (i, 0)),
            pl.BlockSpec((None, 1, d), lsel),
            pl.BlockSpec((None,) + w_in_b.shape[1:], lsel),
        ],
        out_specs=[
            pl.BlockSpec((tm, 2 * A_WIDTH), lambda i: (i, 0)),
            pl.BlockSpec((tm // tk, arows, tk), lambda i: (i, 0, 0)),
            pl.BlockSpec((tm, B_WIDTH), lambda i: (i, 0)),
            pl.BlockSpec((tm, 2 * C_WIDTH), lambda i: (i, 0)),
            pl.BlockSpec((tm // tc, crows, tc), lambda i: (i, 0, 0)),
        ],
        out_shape=[
            jax.ShapeDtypeStruct((n, 2 * A_WIDTH), BF16),
            jax.ShapeDtypeStruct((n // tk, arows, tk), BF16),
            jax.ShapeDtypeStruct((n, B_WIDTH), F32),
            jax.ShapeDtypeStruct((n, 2 * C_WIDTH), BF16),
            jax.ShapeDtypeStruct((n // tc, crows, tc), BF16),
        ],
        scratch_shapes=[
            pltpu.VMEM((A_WIDTH, d), BF16),
            pltpu.VMEM((C_WIDTH, d), BF16),
        ],
        compiler_params=pltpu.CompilerParams(
            dimension_semantics=("arbitrary",), vmem_limit_bytes=VMEM_LIMIT),
        name="in_proj",
    )(x2, attn_norm, w_in_b)


def _attn_a_kernel(lam_ref, gain_ref, q_ref, k_ref, vt_ref, o_ref,
                   qcat_ref, sa_ref, sb_ref, m_ref, l_ref, acc_ref,
                   *, tq, tk, strip, nq, lam_init):
    nstrip = 2 * tq // strip

    def load_queries(i):
        q = q_ref[pl.ds(pl.multiple_of(i * tq, tq), tq), :]
        lane = lax.broadcasted_iota(jnp.int32, q.shape, 1)
        zero = jnp.zeros_like(q)
        qcat_ref[0:tq, :] = jnp.where(lane < HEAD_DIM, q, zero)
        qcat_ref[tq:2 * tq, :] = jnp.where(lane >= HEAD_DIM, q, zero)

    def reset_state():
        m_ref[...] = jnp.full(m_ref.shape, NEG, F32)
        l_ref[...] = jnp.zeros(l_ref.shape, F32)
        acc_ref[...] = jnp.zeros(acc_ref.shape, F32)

    def visible(c, diag):
        qblock = ((c * strip) % tq) // tk
        return 2 if diag is None or qblock > diag else int(qblock == diag)

    def pieces(c, diag):
        kind = visible(c, diag)
        if kind == 0:
            return []
        if kind == 2:
            return [(c * strip, strip, tk, False)]
        out = []
        for h in range(strip // DIAG_SUB):
            off = (c * strip) % tk + h * DIAG_SUB
            out.append((c * strip + h * DIAG_SUB, DIAG_SUB, off + DIAG_SUB, True))
        return out

    def scores(j, c, dst_ref, diag=None):
        for col, width, keys, _ in pieces(c, diag):
            k_t = k_ref[pl.ds(pl.multiple_of(j * tk, tk), keys), :]
            dst_ref[0:keys, col:col + width] = _dot_nt(k_t, qcat_ref[col:col + width, :])

    def consume(t, c, src_ref, diag=None):
        for col, width, keys, masked in pieces(c, diag):
            cs = slice(col, col + width)
            s = src_ref[0:keys, cs]
            if masked:
                kc = lax.broadcasted_iota(jnp.int32, s.shape, 0) // CHUNK
                qc = (col % tk + lax.broadcasted_iota(jnp.int32, s.shape, 1)) // CHUNK
                s = jnp.where(kc <= qc, s, NEG)
            m_old = m_ref[:, cs]
            m_new = jnp.maximum(m_old, jnp.max(s, axis=0, keepdims=True))
            alpha = jnp.exp2(m_old - m_new)
            p = jnp.exp2(s - m_new).astype(BF16)
            pv = jnp.dot(vt_ref[t, :, 0:keys], p, preferred_element_type=F32)
            l_ref[:, cs] = alpha * l_ref[:, cs] + pv[A_V:A_V + 1, :]
            acc_ref[:, cs] = alpha * acc_ref[:, cs] + pv[:A_V, :]
            m_ref[:, cs] = m_new

    def pipelined(j, dst_ref, src_ref, diag_next=None, diag=None):
        for c in range(nstrip):
            scores(j, c, dst_ref, diag_next)
            consume(j - 1, c, src_ref, diag)

    def finalize(i):
        lv = lam_ref[...]
        lam = (jnp.exp(jnp.sum(lv[0:1] * lv[1:2], axis=1, keepdims=True))
               - jnp.exp(jnp.sum(lv[2:3] * lv[3:4], axis=1, keepdims=True)) + lam_init)
        inv = 1.0 / l_ref[...]
        a = (acc_ref[:, :tq] * inv[:, :tq]
             - acc_ref[:, tq:] * (lam * inv[:, tq:]))
        ms = jnp.mean(a * a, axis=0, keepdims=True)
        y = a * lax.rsqrt(ms + EPS) * (gain_ref[...] * (1.0 - lam_init))
        o_ref[pl.ds(pl.multiple_of(i * tq, tq), tq), :] = y.T.astype(o_ref.dtype)

    def pair(mm, carry):
        pipelined(2 * mm - 1, sb_ref, sa_ref)
        pipelined(2 * mm, sa_ref, sb_ref)
        return carry

    ratio = tq // tk
    assert ratio % 4 == 0
    load_queries(0)
    reset_state()
    for c in range(nstrip):
        scores(0, c, sa_ref)

    def query_tile(i, carry):
        first = ratio * i
        lax.fori_loop(0, first // 4, lambda n, cr: pair(2 * n + 2, pair(2 * n + 1, cr)), 0)
        bufs = (sa_ref, sb_ref)
        for r in range(ratio - 1):
            pipelined(first + r + 1, bufs[(r + 1) % 2], bufs[r % 2], diag_next=r + 1, diag=r)
        load_queries(jnp.minimum(i + 1, nq - 1))
        for c in range(nstrip):
            scores(0, c, sa_ref)
            consume(first + ratio - 1, c, sb_ref, diag=ratio - 1)
        finalize(i)
        reset_state()
        return carry

    lax.fori_loop(0, nq, query_tile, 0)


def _attn_a(qk, vt, lamvec, gain, layer, bsz, s_len):
    n = qk.shape[0]
    tq, tk = TQ_A, TK_A
    assert tq % (2 * tk) == 0 and tk % STRIP_A == 0
    nq = s_len // tq
    lam_init = 0.8 - 0.6 * math.exp(-0.3 * layer)
    return pl.pallas_call(
        functools.partial(_attn_a_kernel, tq=tq, tk=tk, strip=STRIP_A, nq=nq, lam_init=lam_init),
        grid=(bsz, A_HEADS),
        in_specs=[
            pl.BlockSpec((None, 4, HEAD_DIM), lambda b, h: (layer, 0, 0)),
            pl.BlockSpec((None, A_V, 1), lambda b, h: (layer, 0, 0)),
            pl.BlockSpec((s_len, A_V), lambda b, h: (b, h)),
            pl.BlockSpec((s_len, A_V), lambda b, h: (b, A_HEADS + h)),
            pl.BlockSpec((s_len // tk, A_V + SUM_ROWS, tk), lambda b, h: (b, h, 0)),
        ],
        out_specs=pl.BlockSpec((s_len, A_V), lambda b, h: (b, h)),
        out_shape=jax.ShapeDtypeStruct((n, A_WIDTH), BF16),
        scratch_shapes=[
            pltpu.VMEM((2 * tq, A_V), BF16),
            pltpu.VMEM((tk, 2 * tq), F32),
            pltpu.VMEM((tk, 2 * tq), F32),
            pltpu.VMEM((1, 2 * tq), F32),
            pltpu.VMEM((1, 2 * tq), F32),
            pltpu.VMEM((A_V, 2 * tq), F32),
        ],
        compiler_params=pltpu.CompilerParams(
            dimension_semantics=("arbitrary", "arbitrary"),
            vmem_limit_bytes=VMEM_LIMIT),
        name="attn_a",
    )(lamvec, gain, qk, qk, vt)


def _mix_bc_kernel(rb_ref, pw_ref, ps_ref, u_ref, up_ref, q_ref,
                   k0_ref, k1_ref, k2_ref, v0_ref, v1_ref, v2_ref,
                   o_ref, bias_ref, s_ref, *, tq):
    j = pl.program_id(1)
    nk = 3 * tq

    @pl.when((pl.program_id(0) == 0) & (j == 0))
    def _build_bias():
        rb = rb_ref[...]
        lane = lax.broadcasted_iota(jnp.int32, rb.shape, 1)
        edge = rb[:, 2 * REL_CLIP:2 * REL_CLIP + 1]
        ext = jnp.where(lane > 2 * REL_CLIP, edge, rb)
        kc = lax.broadcasted_iota(jnp.int32, (nk, tq), 0) // CHUNK
        qc = lax.broadcasted_iota(jnp.int32, (nk, tq), 1) // CHUNK
        band = (kc >= qc) & (kc <= qc + C_LEFT_CHUNKS)
        for h in range(C_HEADS):
            rows = jnp.broadcast_to(ext[h:h + 1, :], (nk, ext.shape[1]))
            rolled = pltpu.roll(rows, REL_CLIP, axis=1, stride=1, stride_axis=0)
            bias_ref[h] = jnp.where(band, rolled[:, :tq] * LOG2E, NEG)

    u = u_ref[...]
    halo = jnp.where(j > 0, up_ref[...], 0.0)
    ext_u = jnp.concatenate([halo, u], axis=0)
    sums, w = {1: ext_u}, 1
    while w < max(POOL_WINDOWS):
        sums[2 * w] = sums[w] + pltpu.roll(sums[w], w, axis=0)
        w *= 2
    grp = lax.broadcasted_iota(jnp.int32, (1, B_WIDTH), 1) // B_GROUP_DIM
    wsum, win = sums[POOL_WINDOWS[-1]], POOL_WINDOWS[-1]
    for g in reversed(range(B_GROUPS - 1)):
        wsum = jnp.where(grp == g, sums[POOL_WINDOWS[g]], wsum)
        win = jnp.where(grp == g, POOL_WINDOWS[g], win)
    wsum = wsum[POOL_HALO:, :]
    pos = j * tq + lax.broadcasted_iota(jnp.int32, (tq, 1), 0)
    cnt = jnp.minimum(pos + 1, win).astype(F32)
    d = wsum / cnt - u
    yb = jnp.dot(d.astype(BF16), pw_ref[...], preferred_element_type=F32) * ps_ref[...]
    o_ref[:, 0:B_WIDTH] = yb.astype(o_ref.dtype)

    q = q_ref[...]
    kk = jnp.concatenate([k0_ref[...], k1_ref[...], k2_ref[...]], axis=0)
    vts = (v0_ref, v1_ref, v2_ref)
    head = lax.broadcasted_iota(jnp.int32, (1, C_WIDTH), 1) // HEAD_DIM
    for h in range(C_HEADS):
        qh = jnp.where(head == h, q, jnp.zeros_like(q))
        s_ref[h] = _dot_nt(kk, qh)
    rows = HEAD_DIM + SUM_ROWS
    krow = lax.broadcasted_iota(jnp.int32, (nk, tq), 0)
    kneg = jnp.where(krow >= (2 - j) * tq, 0.0, NEG)
    outs = []
    for h in range(C_HEADS):
        s = s_ref[h] + bias_ref[h] + kneg
        m = jnp.max(s, axis=0, keepdims=True)
        p = jnp.exp2(s - m).astype(BF16)
        pv = jnp.dot(vts[0][h * rows:(h + 1) * rows, :], p[0:tq, :], preferred_element_type=F32)
        for t in (1, 2):
            pv = pv + jnp.dot(vts[t][h * rows:(h + 1) * rows, :], p[t * tq:(t + 1) * tq, :],
                              preferred_element_type=F32)
        outs.append(pv[:HEAD_DIM, :] / pv[HEAD_DIM:HEAD_DIM + 1, :])
    out = jnp.concatenate(outs, axis=0).T
    o_ref[:, B_WIDTH:B_WIDTH + C_WIDTH] = out.astype(o_ref.dtype)


def _mix_bc(u, c, vct, rbp, pw, ps, layer, bsz, s_len):
    n = u.shape[0]
    tq = TQ_C
    assert 2 * tq == C_LEFT_CHUNKS * CHUNK and rbp.shape[-1] == 4 * tq
    nt = s_len // tq
    hpt = tq // POOL_HALO
    crows = vct.shape[1]
    cur = lambda col: (lambda b, j: (b * nt + j, col))
    prev = lambda back, col: (lambda b, j: (b * nt + jnp.maximum(j - back, 0), col))
    blk = lambda: (tq, C_WIDTH)
    vblk = lambda back: pl.BlockSpec((None, crows, tq),
                                     lambda b, j: (b * nt + jnp.maximum(j - back, 0), 0, 0))
    return pl.pallas_call(
        functools.partial(_mix_bc_kernel, tq=tq),
        grid=(bsz, nt),
        in_specs=[
            pl.BlockSpec((None,) + rbp.shape[1:], lambda b, j: (layer, 0, 0)),
            pl.BlockSpec((None,) + pw.shape[1:], lambda b, j: (layer, 0, 0)),
            pl.BlockSpec((None, 1, B_WIDTH), lambda b, j: (layer, 0, 0)),
            pl.BlockSpec((tq, B_WIDTH), cur(0)),
            pl.BlockSpec((POOL_HALO, B_WIDTH),
                         lambda b, j: ((b * nt + j) * hpt - jnp.minimum(j, 1), 0)),
            pl.BlockSpec(blk(), cur(0)),
            pl.BlockSpec(blk(), prev(2, 1)), pl.BlockSpec(blk(), prev(1, 1)), pl.BlockSpec(blk(), cur(1)),
            vblk(2), vblk(1), vblk(0),
        ],
        out_specs=pl.BlockSpec((tq, B_WIDTH + C_WIDTH), lambda b, j: (b * nt + j, 0)),
        out_shape=jax.ShapeDtypeStruct((n, B_WIDTH + C_WIDTH), BF16),
        scratch_shapes=[pltpu.VMEM((C_HEADS, 3 * tq, tq), F32),
                        pltpu.VMEM((C_HEADS, 3 * tq, tq), F32)],
        compiler_params=pltpu.CompilerParams(
            dimension_semantics=("arbitrary", "arbitrary"), vmem_limit_bytes=VMEM_LIMIT),
        name="mix_bc",
    )(rbp, pw, ps, u, u, c, c, c, c, vct, vct, vct)


def _post_kernel(x_ref, a_ref, bc_ref, wo_ref, g_ref, wup_ref, cw_ref, cb_ref, wd_ref, gf_ref,
                 o_ref, h_ref, acc_ref, carry_ref, a0_ref, g0_ref, a1_ref, g1_ref, hid_ref,
                 *, tiles_per_seq, final_norm):
    i = pl.program_id(0)
    tm = x_ref.shape[0]
    d_ff = wd_ref.shape[0]
    fc = a0_ref.shape[1]
    nch = d_ff // fc

    def cols(c, base=0):
        start = base + c * fc
        return pl.ds(start if isinstance(start, int) else pl.multiple_of(start, fc), fc)

    xm = (x_ref[...]
          + jnp.dot(a_ref[...], wo_ref[0:A_WIDTH, :], preferred_element_type=F32)
          + jnp.dot(bc_ref[...], wo_ref[A_WIDTH:, :], preferred_element_type=F32))
    h_ref[...] = _rms(xm, g_ref[...]).astype(BF16)
    acc_ref[...] = xm
    seq_start = (i % tiles_per_seq) == 0
    row = lax.broadcasted_iota(jnp.int32, (8, 1), 0)

    def up(c, a_ref, g_ref):
        h = h_ref[...]
        a_ref[...] = jnp.dot(h, wup_ref[:, cols(c)], preferred_element_type=F32)
        g_ref[...] = jnp.dot(h, wup_ref[:, cols(c, d_ff)], preferred_element_type=F32)

    def params(c):
        tail = jnp.where(seq_start, 0.0, carry_ref[c])
        return tail, cw_ref[:, cols(c)], cb_ref[:, cols(c)]

    def gate(c, a_ref, g_ref, prm):
        tail, cw, cb = prm
        a = a_ref[...]
        carry_ref[c] = a[tm - 8:, :]
        r1 = pltpu.roll(a, 1, axis=0)
        r2 = pltpu.roll(a, 2, axis=0)
        top1 = jnp.where(row == 0, tail[7:8, :], r1[:8, :])
        top2 = jnp.where(row == 0, tail[6:7, :], jnp.where(row == 1, tail[7:8, :], r2[:8, :]))
        a1 = jnp.concatenate([top1, r1[8:, :]], axis=0)
        a2 = jnp.concatenate([top2, r2[8:, :]], axis=0)
        conv = cb + cw[0:1, :] * a2
        conv = conv + cw[1:2, :] * a1
        conv = conv + cw[2:3, :] * a
        hid_ref[:, cols(c)] = (conv * jax.nn.sigmoid(conv) * g_ref[...]).astype(BF16)

    assert nch % 2 == 1 and nch >= 3
    up(0, a0_ref, g0_ref)
    up(1, a1_ref, g1_ref)

    def pair(pp, carry):
        c = 2 * pp
        prm0, prm1 = params(c), params(c + 1)
        gate(c, a0_ref, g0_ref, prm0)
        up(c + 2, a0_ref, g0_ref)
        gate(c + 1, a1_ref, g1_ref, prm1)
        up(c + 3, a1_ref, g1_ref)
        return carry

    npair = (nch - 3) // 2
    assert npair % 2 == 0
    for pp in range(npair):
        pair(pp, 0)
    gate(nch - 3, a0_ref, g0_ref, params(nch - 3))
    up(nch - 1, a0_ref, g0_ref)
    gate(nch - 2, a1_ref, g1_ref, params(nch - 2))
    gate(nch - 1, a0_ref, g0_ref, params(nch - 1))
    out = acc_ref[...] + jnp.dot(hid_ref[...], wd_ref[...], preferred_element_type=F32)
    if final_norm:
        out = _rms(out, gf_ref[...])
    o_ref[...] = out


def _post(x2, oa, obc, wo, ffn_norm, wup, cw, cb, wd, gf, layer, s_len, final_norm):
    n, d = x2.shape
    tm, fc = TM_POST, FF_CHUNK
    d_ff = wd.shape[1]
    nch = d_ff // fc
    const = pl.Buffered(1)
    l3 = lambda i: (layer, 0, 0)
    return pl.pallas_call(
        functools.partial(_post_kernel, tiles_per_seq=s_len // tm, final_norm=final_norm),
        grid=(n // tm,),
        in_specs=[
            pl.BlockSpec((tm, d), lambda i: (i, 0)),
            pl.BlockSpec((tm, A_WIDTH), lambda i: (i, 0)),
            pl.BlockSpec((tm, B_WIDTH + C_WIDTH), lambda i: (i, 0)),
            pl.BlockSpec((None,) + wo.shape[1:], l3, pipeline_mode=const),
            pl.BlockSpec((None, 1, d), l3),
            pl.BlockSpec((None,) + wup.shape[1:], l3, pipeline_mode=const),
            pl.BlockSpec((None,) + cw.shape[1:], l3),
            pl.BlockSpec((None,) + cb.shape[1:], l3),
            pl.BlockSpec((None,) + wd.shape[1:], l3, pipeline_mode=const),
            pl.BlockSpec((1, d), lambda i: (0, 0)),
        ],
        out_specs=pl.BlockSpec((tm, d), lambda i: (i, 0)),
        out_shape=jax.ShapeDtypeStruct((n, d), F32),
        scratch_shapes=[
            pltpu.VMEM((tm, d), BF16),
            pltpu.VMEM((tm, d), F32),
            pltpu.VMEM((nch, 8, fc), F32),
            pltpu.VMEM((tm, fc), F32), pltpu.VMEM((tm, fc), F32),
            pltpu.VMEM((tm, fc), F32), pltpu.VMEM((tm, fc), F32),
            pltpu.VMEM((tm, d_ff), BF16),
        ],
        compiler_params=pltpu.CompilerParams(
            dimension_semantics=("arbitrary",), vmem_limit_bytes=VMEM_LIMIT),
        name="post",
    )(x2, oa, obc, wo, ffn_norm, wup, cw, cb, wd, gf)


def kernel(x, attn_norm, w_in, lam_q1, lam_k1, lam_q2, lam_k2, diff_subln, pool_w, pool_scale,
           rel_bias, w_out, ffn_norm, w_up, conv_w, conv_b, w_down, final_norm):
    bsz, s_len, d = x.shape
    depth = w_in.shape[0]
    d_ff = w_down.shape[1]
    assert d_ff % FF_CHUNK == 0 and s_len % TQ_A == 0 and s_len % TM_POST == 0
    assert all(w & (w - 1) == 0 and w <= POOL_HALO for w in POOL_WINDOWS) and len(POOL_WINDOWS) == B_GROUPS

    w_in_b = w_in.astype(BF16)
    wo = w_out.astype(BF16)
    wup = w_up.astype(BF16)
    wd = w_down.astype(BF16)
    cw = conv_w
    cb = conv_b[:, None, :]
    lamvec = jnp.stack([lam_q1, lam_k1, lam_q2, lam_k2], axis=1)
    gain = diff_subln[:, :, None]
    eye = jnp.eye(B_GROUPS, dtype=pool_w.dtype)
    pw = jnp.einsum('lgcd,gh->lgchd', pool_w, eye).reshape(depth, B_WIDTH, B_WIDTH).astype(BF16)
    ps = pool_scale[:, None, :]
    rbp = jnp.pad(rel_bias, ((0, 0), (0, 0), (0, 4 * TQ_C - rel_bias.shape[-1])))
    an = attn_norm[:, None, :]
    fn = ffn_norm[:, None, :]
    gf = final_norm[None, :]

    x2 = x.reshape(bsz * s_len, d)
    for layer in range(depth):
        qk, vt, u, c, vct = _in_proj(x2, an, w_in_b, layer)
        oa = _attn_a(qk, vt, lamvec, gain, layer, bsz, s_len)
        obc = _mix_bc(u, c, vct, rbp, pw, ps, layer, bsz, s_len)
        x2 = _post(x2, oa, obc, wo, fn, wup, cw, cb, wd, gf, layer, s_len,
                   final_norm=(layer == depth - 1))
    return x2.reshape(bsz, s_len, d)
```

```python
import functools
import math

import jax
import jax.numpy as jnp
from jax import lax
from jax.experimental import pallas as pl
from jax.experimental.pallas import tpu as pltpu

F32 = jnp.float32
BF16 = jnp.bfloat16

CHUNK = 64
HEAD_DIM = 64
A_HEADS = 4
A_V = 2 * HEAD_DIM
A_WIDTH = A_HEADS * A_V
B_GROUPS = 4
B_GROUP_DIM = 64
B_WIDTH = B_GROUPS * B_GROUP_DIM
POOL_WINDOWS = (2, 4, 8, 16)
C_HEADS = 4
C_WIDTH = C_HEADS * HEAD_DIM
C_LEFT_CHUNKS = 8
REL_CLIP = 256
EPS = 1e-5
NEG = -1e30
LOG2E = math.log2(math.e)
SUM_ROWS = 16

TM_PROJ = 1024
TQ_A = 2048
TK_A = 512
STRIP_A = 512
DIAG_SUB = 256
TQ_C = 256
TM_POST = 512
FF_CHUNK = 256
POOL_HALO = 16
VMEM_LIMIT = 56 * 1024 * 1024


def _rms(x, g):
    ms = jnp.mean(x * x, axis=-1, keepdims=True)
    return x * lax.rsqrt(ms + EPS) * g


def _dot_nt(a, b):
    return lax.dot_general(a, b, (((1,), (1,)), ((), ())), preferred_element_type=F32)


def _in_proj_kernel(x_ref, g_ref, w_ref, qk_ref, vt_ref, u_ref, c_ref, vct_ref, wvt_ref, wvct_ref,
                    *, tk, tc):
    v0 = 2 * A_WIDTH
    u0 = 3 * A_WIDTH
    c0 = u0 + B_WIDTH
    vc0 = c0 + 2 * C_WIDTH

    @pl.when(pl.program_id(0) == 0)
    def _transpose_value_weights():
        wvt_ref[...] = w_ref[:, v0:u0].astype(F32).T.astype(BF16)
        wvct_ref[...] = w_ref[:, vc0:].astype(F32).T.astype(BF16)

    h = _rms(x_ref[...], g_ref[...]).astype(BF16)
    qk = jnp.dot(h, w_ref[:, :v0], preferred_element_type=F32)
    col = lax.broadcasted_iota(jnp.int32, (1, qk.shape[1]), 1)
    qscale = HEAD_DIM ** -0.5 * LOG2E
    qk_ref[...] = (qk * jnp.where(col < A_WIDTH, qscale, 1.0)).astype(BF16)
    vt = _dot_nt(wvt_ref[...], h).astype(BF16)
    rows = A_V + SUM_ROWS
    for t in range(vt_ref.shape[0]):
        for hd in range(A_HEADS):
            vt_ref[t, hd * rows:hd * rows + A_V, :] = vt[hd * A_V:(hd + 1) * A_V, t * tk:(t + 1) * tk]
            vt_ref[t, hd * rows + A_V:(hd + 1) * rows, :] = jnp.ones((SUM_ROWS, tk), BF16)
    u_ref[...] = jnp.dot(h, w_ref[:, u0:c0], preferred_element_type=F32)
    c = jnp.dot(h, w_ref[:, c0:vc0], preferred_element_type=F32)
    colc = lax.broadcasted_iota(jnp.int32, (1, c.shape[1]), 1)
    c_ref[...] = (c * jnp.where(colc < C_WIDTH, qscale, 1.0)).astype(BF16)
    vct = _dot_nt(wvct_ref[...], h).astype(BF16)
    rows = HEAD_DIM + SUM_ROWS
    for t in range(vct_ref.shape[0]):
        for hd in range(C_HEADS):
            vct_ref[t, hd * rows:hd * rows + HEAD_DIM, :] = (
                vct[hd * HEAD_DIM:(hd + 1) * HEAD_DIM, t * tc:(t + 1) * tc])
            vct_ref[t, hd * rows + HEAD_DIM:(hd + 1) * rows, :] = jnp.ones((SUM_ROWS, tc), BF16)


def _in_proj(x2, attn_norm, w_in_b, layer):
    n, d = x2.shape
    tm, tk, tc = TM_PROJ, TK_A, TQ_C
    lsel = lambda *_: (layer, 0, 0)
    arows = A_HEADS * (A_V + SUM_ROWS)
    crows = C_HEADS * (HEAD_DIM + SUM_ROWS)
    return pl.pallas_call(
        functools.partial(_in_proj_kernel, tk=tk, tc=tc),
        grid=(n // tm,),
        in_specs=[
            pl.BlockSpec((tm, d), lambda i: (i, 0)),
            pl.BlockSpec((None, 1, d), lsel),
            pl.BlockSpec((None,) + w_in_b.shape[1:], lsel),
        ],
        out_specs=[
            pl.BlockSpec((tm, 2 * A_WIDTH), lambda i: (i, 0)),
            pl.BlockSpec((tm // tk, arows, tk), lambda i: (i, 0, 0)),
            pl.BlockSpec((tm, B_WIDTH), lambda i: (i, 0)),
            pl.BlockSpec((tm, 2 * C_WIDTH), lambda i: (i, 0)),
            pl.BlockSpec((tm // tc, crows, tc), lambda i: (i, 0, 0)),
        ],
        out_shape=[
            jax.ShapeDtypeStruct((n, 2 * A_WIDTH), BF16),
            jax.ShapeDtypeStruct((n // tk, arows, tk), BF16),
            jax.ShapeDtypeStruct((n, B_WIDTH), F32),
            jax.ShapeDtypeStruct((n, 2 * C_WIDTH), BF16),
            jax.ShapeDtypeStruct((n // tc, crows, tc), BF16),
        ],
        scratch_shapes=[
            pltpu.VMEM((A_WIDTH, d), BF16),
            pltpu.VMEM((C_WIDTH, d), BF16),
        ],
        compiler_params=pltpu.CompilerParams(
            dimension_semantics=("arbitrary",), vmem_limit_bytes=VMEM_LIMIT),
        name="in_proj",
    )(x2, attn_norm, w_in_b)


def _attn_a_kernel(lam_ref, gain_ref, q_ref, k_ref, vt_ref, o_ref,
                   qcat_ref, sa_ref, sb_ref, m_ref, l_ref, acc_ref,
                   *, tq, tk, strip, nq, lam_init):
    nstrip = 2 * tq // strip

    def load_queries(i):
        q = q_ref[pl.ds(pl.multiple_of(i * tq, tq), tq), :]
        lane = lax.broadcasted_iota(jnp.int32, q.shape, 1)
        zero = jnp.zeros_like(q)
        qcat_ref[0:tq, :] = jnp.where(lane < HEAD_DIM, q, zero)
        qcat_ref[tq:2 * tq, :] = jnp.where(lane >= HEAD_DIM, q, zero)

    def reset_state():
        m_ref[...] = jnp.full(m_ref.shape, NEG, F32)
        l_ref[...] = jnp.zeros(l_ref.shape, F32)
        acc_ref[...] = jnp.zeros(acc_ref.shape, F32)

    def visible(c, diag):
        qblock = ((c * strip) % tq) // tk
        return 2 if diag is None or qblock > diag else int(qblock == diag)

    def pieces(c, diag):
        kind = visible(c, diag)
        if kind == 0:
            return []
        if kind == 2:
            return [(c * strip, strip, tk, False)]
        out = []
        for h in range(strip // DIAG_SUB):
            off = (c * strip) % tk + h * DIAG_SUB
            out.append((c * strip + h * DIAG_SUB, DIAG_SUB, off + DIAG_SUB, True))
        return out

    def scores(j, c, dst_ref, diag=None):
        for col, width, keys, _ in pieces(c, diag):
            k_t = k_ref[pl.ds(pl.multiple_of(j * tk, tk), keys), :]
            dst_ref[0:keys, col:col + width] = _dot_nt(k_t, qcat_ref[col:col + width, :])

    def consume(t, c, src_ref, diag=None):
        for col, width, keys, masked in pieces(c, diag):
            cs = slice(col, col + width)
            s = src_ref[0:keys, cs]
            if masked:
                kc = lax.broadcasted_iota(jnp.int32, s.shape, 0) // CHUNK
                qc = (col % tk + lax.broadcasted_iota(jnp.int32, s.shape, 1)) // CHUNK
                s = jnp.where(kc <= qc, s, NEG)
            m_old = m_ref[:, cs]
            m_new = jnp.maximum(m_old, jnp.max(s, axis=0, keepdims=True))
            alpha = jnp.exp2(m_old - m_new)
            p = jnp.exp2(s - m_new).astype(BF16)
            pv = jnp.dot(vt_ref[t, :, 0:keys], p, preferred_element_type=F32)
            l_ref[:, cs] = alpha * l_ref[:, cs] + pv[A_V:A_V + 1, :]
            acc_ref[:, cs] = alpha * acc_ref[:, cs] + pv[:A_V, :]
            m_ref[:, cs] = m_new

    def pipelined(j, dst_ref, src_ref, diag_next=None, diag=None):
        for c in range(nstrip):
            scores(j, c, dst_ref, diag_next)
            consume(j - 1, c, src_ref, diag)

    def finalize(i):
        lv = lam_ref[...]
        lam = (jnp.exp(jnp.sum(lv[0:1] * lv[1:2], axis=1, keepdims=True))
               - jnp.exp(jnp.sum(lv[2:3] * lv[3:4], axis=1, keepdims=True)) + lam_init)
        inv = 1.0 / l_ref[...]
        a = (acc_ref[:, :tq] * inv[:, :tq]
             - acc_ref[:, tq:] * (lam * inv[:, tq:]))
        ms = jnp.mean(a * a, axis=0, keepdims=True)
        y = a * lax.rsqrt(ms + EPS) * (gain_ref[...] * (1.0 - lam_init))
        o_ref[pl.ds(pl.multiple_of(i * tq, tq), tq), :] = y.T.astype(o_ref.dtype)

    def pair(mm, carry):
        pipelined(2 * mm - 1, sb_ref, sa_ref)
        pipelined(2 * mm, sa_ref, sb_ref)
        return carry

    ratio = tq // tk
    assert ratio % 4 == 0
    load_queries(0)
    reset_state()
    for c in range(nstrip):
        scores(0, c, sa_ref)

    def query_tile(i, carry):
        first = ratio * i
        lax.fori_loop(0, first // 4, lambda n, cr: pair(2 * n + 2, pair(2 * n + 1, cr)), 0)
        bufs = (sa_ref, sb_ref)
        for r in range(ratio - 1):
            pipelined(first + r + 1, bufs[(r + 1) % 2], bufs[r % 2], diag_next=r + 1, diag=r)
        load_queries(jnp.minimum(i + 1, nq - 1))
        for c in range(nstrip):
            scores(0, c, sa_ref)
            consume(first + ratio - 1, c, sb_ref, diag=ratio - 1)
        finalize(i)
        reset_state()
        return carry

    lax.fori_loop(0, nq, query_tile, 0)


def _attn_a(qk, vt, lamvec, gain, layer, bsz, s_len):
    n = qk.shape[0]
    tq, tk = TQ_A, TK_A
    assert tq % (2 * tk) == 0 and tk % STRIP_A == 0
    nq = s_len // tq
    lam_init = 0.8 - 0.6 * math.exp(-0.3 * layer)
    return pl.pallas_call(
        functools.partial(_attn_a_kernel, tq=tq, tk=tk, strip=STRIP_A, nq=nq, lam_init=lam_init),
        grid=(bsz, A_HEADS),
        in_specs=[
            pl.BlockSpec((None, 4, HEAD_DIM), lambda b, h: (layer, 0, 0)),
            pl.BlockSpec((None, A_V, 1), lambda b, h: (layer, 0, 0)),
            pl.BlockSpec((s_len, A_V), lambda b, h: (b, h)),
            pl.BlockSpec((s_len, A_V), lambda b, h: (b, A_HEADS + h)),
            pl.BlockSpec((s_len // tk, A_V + SUM_ROWS, tk), lambda b, h: (b, h, 0)),
        ],
        out_specs=pl.BlockSpec((s_len, A_V), lambda b, h: (b, h)),
        out_shape=jax.ShapeDtypeStruct((n, A_WIDTH), BF16),
        scratch_shapes=[
            pltpu.VMEM((2 * tq, A_V), BF16),
            pltpu.VMEM((tk, 2 * tq), F32),
            pltpu.VMEM((tk, 2 * tq), F32),
            pltpu.VMEM((1, 2 * tq), F32),
            pltpu.VMEM((1, 2 * tq), F32),
            pltpu.VMEM((A_V, 2 * tq), F32),
        ],
        compiler_params=pltpu.CompilerParams(
            dimension_semantics=("arbitrary", "arbitrary"),
            vmem_limit_bytes=VMEM_LIMIT),
        name="attn_a",
    )(lamvec, gain, qk, qk, vt)


def _mix_bc_kernel(rb_ref, pw_ref, ps_ref, u_ref, up_ref, q_ref, kp_ref, kc_ref, vp_ref, vc_ref,
                   o_ref, bias_ref, s_ref, *, tq, sub):
    j = pl.program_id(1)
    nk = 3 * tq
    rows_u = sub * tq

    @pl.when((pl.program_id(0) == 0) & (j == 0))
    def _build_bias():
        rb = rb_ref[...]
        lane = lax.broadcasted_iota(jnp.int32, rb.shape, 1)
        edge = rb[:, 2 * REL_CLIP:2 * REL_CLIP + 1]
        ext = jnp.where(lane > 2 * REL_CLIP, edge, rb)
        kc = lax.broadcasted_iota(jnp.int32, (nk, tq), 0) // CHUNK
        qc = lax.broadcasted_iota(jnp.int32, (nk, tq), 1) // CHUNK
        band = (kc >= qc) & (kc <= qc + C_LEFT_CHUNKS)
        for h in range(C_HEADS):
            rows = jnp.broadcast_to(ext[h:h + 1, :], (nk, ext.shape[1]))
            rolled = pltpu.roll(rows, REL_CLIP, axis=1, stride=1, stride_axis=0)
            bias_ref[h] = jnp.where(band, rolled[:, :tq] * LOG2E, NEG)

    u = u_ref[...]
    halo = jnp.where(j > 0, up_ref[...], 0.0)
    ext_u = jnp.concatenate([halo, u], axis=0)
    sums, w = {1: ext_u}, 1
    while w < max(POOL_WINDOWS):
        sums[2 * w] = sums[w] + pltpu.roll(sums[w], w, axis=0)
        w *= 2
    grp = lax.broadcasted_iota(jnp.int32, (1, B_WIDTH), 1) // B_GROUP_DIM
    wsum, win = sums[POOL_WINDOWS[-1]], POOL_WINDOWS[-1]
    for g in reversed(range(B_GROUPS - 1)):
        wsum = jnp.where(grp == g, sums[POOL_WINDOWS[g]], wsum)
        win = jnp.where(grp == g, POOL_WINDOWS[g], win)
    wsum = wsum[POOL_HALO:, :]
    pos = j * rows_u + lax.broadcasted_iota(jnp.int32, (rows_u, 1), 0)
    cnt = jnp.minimum(pos + 1, win).astype(F32)
    d = wsum / cnt - u
    yb = jnp.dot(d.astype(BF16), pw_ref[...], preferred_element_type=F32) * ps_ref[...]
    o_ref[:, 0:B_WIDTH] = yb.astype(o_ref.dtype)

    assert sub == 2
    key_tiles = [kp_ref[0:tq, :], kp_ref[tq:2 * tq, :], kc_ref[0:tq, :], kc_ref[tq:2 * tq, :]]
    val_tiles = [(vp_ref, 0), (vp_ref, 1), (vc_ref, 0), (vc_ref, 1)]
    head = lax.broadcasted_iota(jnp.int32, (1, C_WIDTH), 1) // HEAD_DIM
    for t in range(sub):
        q = q_ref[t * tq:(t + 1) * tq, :]
        kk = jnp.concatenate(key_tiles[t:t + 3], axis=0)
        for h in range(C_HEADS):
            qh = jnp.where(head == h, q, jnp.zeros_like(q))
            s_ref[t * C_HEADS + h] = _dot_nt(kk, qh)
    rows = HEAD_DIM + SUM_ROWS
    krow = lax.broadcasted_iota(jnp.int32, (nk, tq), 0)
    for t in range(sub):
        kneg = jnp.where(krow >= (2 - (sub * j + t)) * tq, 0.0, NEG)
        outs = []
        for h in range(C_HEADS):
            s = s_ref[t * C_HEADS + h] + bias_ref[h] + kneg
            m = jnp.max(s, axis=0, keepdims=True)
            p = jnp.exp2(s - m).astype(BF16)
            pv = None
            for w in range(3):
                vref, vi = val_tiles[t + w]
                part = jnp.dot(vref[vi, h * rows:(h + 1) * rows, :], p[w * tq:(w + 1) * tq, :],
                               preferred_element_type=F32)
                pv = part if pv is None else pv + part
            outs.append(pv[:HEAD_DIM, :] / pv[HEAD_DIM:HEAD_DIM + 1, :])
        out = jnp.concatenate(outs, axis=0).T
        o_ref[t * tq:(t + 1) * tq, B_WIDTH:B_WIDTH + C_WIDTH] = out.astype(o_ref.dtype)


def _mix_bc(u, c, vct, rbp, pw, ps, layer, bsz, s_len):
    n = u.shape[0]
    tq, sub = TQ_C, 2
    assert 2 * tq == C_LEFT_CHUNKS * CHUNK and rbp.shape[-1] == 4 * tq
    rows = sub * tq
    nt = s_len // rows
    hpt = rows // POOL_HALO
    crows = vct.shape[1]
    cur = lambda col: (lambda b, j: (b * nt + j, col))
    prev = lambda col: (lambda b, j: (b * nt + jnp.maximum(j - 1, 0), col))
    return pl.pallas_call(
        functools.partial(_mix_bc_kernel, tq=tq, sub=sub),
        grid=(bsz, nt),
        in_specs=[
            pl.BlockSpec((None,) + rbp.shape[1:], lambda b, j: (layer, 0, 0)),
            pl.BlockSpec((None,) + pw.shape[1:], lambda b, j: (layer, 0, 0)),
            pl.BlockSpec((None, 1, B_WIDTH), lambda b, j: (layer, 0, 0)),
            pl.BlockSpec((rows, B_WIDTH), cur(0)),
            pl.BlockSpec((POOL_HALO, B_WIDTH),
                         lambda b, j: ((b * nt + j) * hpt - jnp.minimum(j, 1), 0)),
            pl.BlockSpec((rows, C_WIDTH), cur(0)),
            pl.BlockSpec((rows, C_WIDTH), prev(1)),
            pl.BlockSpec((rows, C_WIDTH), cur(1)),
            pl.BlockSpec((sub, crows, tq), lambda b, j: (b * nt + jnp.maximum(j - 1, 0), 0, 0)),
            pl.BlockSpec((sub, crows, tq), lambda b, j: (b * nt + j, 0, 0)),
        ],
        out_specs=pl.BlockSpec((rows, B_WIDTH + C_WIDTH), lambda b, j: (b * nt + j, 0)),
        out_shape=jax.ShapeDtypeStruct((n, B_WIDTH + C_WIDTH), BF16),
        scratch_shapes=[pltpu.VMEM((C_HEADS, 3 * tq, tq), F32),
                        pltpu.VMEM((sub * C_HEADS, 3 * tq, tq), F32)],
        compiler_params=pltpu.CompilerParams(
            dimension_semantics=("arbitrary", "arbitrary"), vmem_limit_bytes=VMEM_LIMIT),
        name="mix_bc",
    )(rbp, pw, ps, u, u, c, c, c, vct, vct)


def _post_kernel(x_ref, a_ref, bc_ref, wo_ref, g_ref, wup_ref, cw_ref, cb_ref, wd_ref, gf_ref,
                 o_ref, h_ref, acc_ref, carry_ref, a0_ref, g0_ref, a1_ref, g1_ref, hid_ref,
                 *, tiles_per_seq, final_norm):
    i = pl.program_id(0)
    tm = x_ref.shape[0]
    d_ff = wd_ref.shape[0]
    fc = a0_ref.shape[1]
    nch = d_ff // fc

    def cols(c, base=0):
        start = base + c * fc
        return pl.ds(start if isinstance(start, int) else pl.multiple_of(start, fc), fc)

    xm = (x_ref[...]
          + jnp.dot(a_ref[...], wo_ref[0:A_WIDTH, :], preferred_element_type=F32)
          + jnp.dot(bc_ref[...], wo_ref[A_WIDTH:, :], preferred_element_type=F32))
    h_ref[...] = _rms(xm, g_ref[...]).astype(BF16)
    acc_ref[...] = xm
    seq_start = (i % tiles_per_seq) == 0
    row = lax.broadcasted_iota(jnp.int32, (8, 1), 0)

    def up(c, a_ref, g_ref):
        h = h_ref[...]
        a_ref[...] = jnp.dot(h, wup_ref[:, cols(c)], preferred_element_type=F32)
        g_ref[...] = jnp.dot(h, wup_ref[:, cols(c, d_ff)], preferred_element_type=F32)

    def params(c):
        tail = jnp.where(seq_start, 0.0, carry_ref[c])
        return tail, cw_ref[:, cols(c)], cb_ref[:, cols(c)]

    def gate(c, a_ref, g_ref, prm):
        tail, cw, cb = prm
        a = a_ref[...]
        carry_ref[c] = a[tm - 8:, :]
        r1 = pltpu.roll(a, 1, axis=0)
        r2 = pltpu.roll(a, 2, axis=0)
        top1 = jnp.where(row == 0, tail[7:8, :], r1[:8, :])
        top2 = jnp.where(row == 0, tail[6:7, :], jnp.where(row == 1, tail[7:8, :], r2[:8, :]))
        a1 = jnp.concatenate([top1, r1[8:, :]], axis=0)
        a2 = jnp.concatenate([top2, r2[8:, :]], axis=0)
        conv = cb + cw[0:1, :] * a2
        conv = conv + cw[1:2, :] * a1
        conv = conv + cw[2:3, :] * a
        hid_ref[:, cols(c)] = (conv * jax.nn.sigmoid(conv) * g_ref[...]).astype(BF16)

    assert nch % 2 == 1 and nch >= 3
    up(0, a0_ref, g0_ref)
    up(1, a1_ref, g1_ref)

    def pair(pp, carry):
        c = 2 * pp
        prm0, prm1 = params(c), params(c + 1)
        gate(c, a0_ref, g0_ref, prm0)
        up(c + 2, a0_ref, g0_ref)
        gate(c + 1, a1_ref, g1_ref, prm1)
        up(c + 3, a1_ref, g1_ref)
        return carry

    npair = (nch - 3) // 2
    assert npair % 2 == 0
    for pp in range(npair):
        pair(pp, 0)
    gate(nch - 3, a0_ref, g0_ref, params(nch - 3))
    up(nch - 1, a0_ref, g0_ref)
    gate(nch - 2, a1_ref, g1_ref, params(nch - 2))
    gate(nch - 1, a0_ref, g0_ref, params(nch - 1))
    out = acc_ref[...] + jnp.dot(hid_ref[...], wd_ref[...], preferred_element_type=F32)
    if final_norm:
        out = _rms(out, gf_ref[...])
    o_ref[...] = out


def _post(x2, oa, obc, wo, ffn_norm, wup, cw, cb, wd, gf, layer, s_len, final_norm):
    n, d = x2.shape
    tm, fc = TM_POST, FF_CHUNK
    d_ff = wd.shape[1]
    nch = d_ff // fc
    const = pl.Buffered(1)
    l3 = lambda i: (layer, 0, 0)
    return pl.pallas_call(
        functools.partial(_post_kernel, tiles_per_seq=s_len // tm, final_norm=final_norm),
        grid=(n // tm,),
        in_specs=[
            pl.BlockSpec((tm, d), lambda i: (i, 0)),
            pl.BlockSpec((tm, A_WIDTH), lambda i: (i, 0)),
            pl.BlockSpec((tm, B_WIDTH + C_WIDTH), lambda i: (i, 0)),
            pl.BlockSpec((None,) + wo.shape[1:], l3, pipeline_mode=const),
            pl.BlockSpec((None, 1, d), l3),
            pl.BlockSpec((None,) + wup.shape[1:], l3, pipeline_mode=const),
            pl.BlockSpec((None,) + cw.shape[1:], l3),
            pl.BlockSpec((None,) + cb.shape[1:], l3),
            pl.BlockSpec((None,) + wd.shape[1:], l3, pipeline_mode=const),
            pl.BlockSpec((1, d), lambda i: (0, 0)),
        ],
        out_specs=pl.BlockSpec((tm, d), lambda i: (i, 0)),
        out_shape=jax.ShapeDtypeStruct((n, d), F32),
        scratch_shapes=[
            pltpu.VMEM((tm, d), BF16),
            pltpu.VMEM((tm, d), F32),
            pltpu.VMEM((nch, 8, fc), F32),
            pltpu.VMEM((tm, fc), F32), pltpu.VMEM((tm, fc), F32),
            pltpu.VMEM((tm, fc), F32), pltpu.VMEM((tm, fc), F32),
            pltpu.VMEM((tm, d_ff), BF16),
        ],
        compiler_params=pltpu.CompilerParams(
            dimension_semantics=("arbitrary",), vmem_limit_bytes=VMEM_LIMIT),
        name="post",
    )(x2, oa, obc, wo, ffn_norm, wup, cw, cb, wd, gf)


def kernel(x, attn_norm, w_in, lam_q1, lam_k1, lam_q2, lam_k2, diff_subln, pool_w, pool_scale,
           rel_bias, w_out, ffn_norm, w_up, conv_w, conv_b, w_down, final_norm):
    bsz, s_len, d = x.shape
    depth = w_in.shape[0]
    d_ff = w_down.shape[1]
    assert d_ff % FF_CHUNK == 0 and s_len % TQ_A == 0 and s_len % TM_POST == 0
    assert all(w & (w - 1) == 0 and w <= POOL_HALO for w in POOL_WINDOWS) and len(POOL_WINDOWS) == B_GROUPS

    w_in_b = w_in.astype(BF16)
    wo = w_out.astype(BF16)
    wup = w_up.astype(BF16)
    wd = w_down.astype(BF16)
    cw = conv_w
    cb = conv_b[:, None, :]
    lamvec = jnp.stack([lam_q1, lam_k1, lam_q2, lam_k2], axis=1)
    gain = diff_subln[:, :, None]
    eye = jnp.eye(B_GROUPS, dtype=pool_w.dtype)
    pw = jnp.einsum('lgcd,gh->lgchd', pool_w, eye).reshape(depth, B_WIDTH, B_WIDTH).astype(BF16)
    ps = pool_scale[:, None, :]
    rbp = jnp.pad(rel_bias, ((0, 0), (0, 0), (0, 4 * TQ_C - rel_bias.shape[-1])))
    an = attn_norm[:, None, :]
    fn = ffn_norm[:, None, :]
    gf = final_norm[None, :]

    x2 = x.reshape(bsz * s_len, d)
    for layer in range(depth):
        qk, vt, u, c, vct = _in_proj(x2, an, w_in_b, layer)
        oa = _attn_a(qk, vt, lamvec, gain, layer, bsz, s_len)
        obc = _mix_bc(u, c, vct, rbp, pw, ps, layer, bsz, s_len)
        x2 = _post(x2, oa, obc, wo, fn, wup, cw, cb, wd, gf, layer, s_len,
                   final_norm=(layer == depth - 1))
    return x2.reshape(bsz, s_len, d)
```

```python
import functools
import math

import jax
import jax.numpy as jnp
from jax import lax
from jax.experimental import pallas as pl
from jax.experimental.pallas import tpu as pltpu

F32 = jnp.float32
BF16 = jnp.bfloat16

CHUNK = 64
HEAD_DIM = 64
A_HEADS = 4
A_V = 2 * HEAD_DIM
A_WIDTH = A_HEADS * A_V
B_GROUPS = 4
B_GROUP_DIM = 64
B_WIDTH = B_GROUPS * B_GROUP_DIM
POOL_WINDOWS = (2, 4, 8, 16)
C_HEADS = 4
C_WIDTH = C_HEADS * HEAD_DIM
C_LEFT_CHUNKS = 8
REL_CLIP = 256
EPS = 1e-5
NEG = -1e30
LOG2E = math.log2(math.e)
SUM_ROWS = 16

TM_PROJ = 1024
TQ_A = 2048
TK_A = 512
STRIP_A = 512
DIAG_SUB = 256
TQ_C = 256
TM_POST = 512
FF_CHUNK = 256
POOL_HALO = 16
VMEM_LIMIT = 56 * 1024 * 1024


def _rms(x, g):
    ms = jnp.mean(x * x, axis=-1, keepdims=True)
    return x * lax.rsqrt(ms + EPS) * g


def _dot_nt(a, b):
    return lax.dot_general(a, b, (((1,), (1,)), ((), ())), preferred_element_type=F32)


def _in_proj_kernel(x_ref, g_ref, w_ref, qk_ref, vt_ref, u_ref, c_ref, vct_ref, wvt_ref, wvct_ref,
                    *, tk, tc):
    v0 = 2 * A_WIDTH
    u0 = 3 * A_WIDTH
    c0 = u0 + B_WIDTH
    vc0 = c0 + 2 * C_WIDTH

    @pl.when(pl.program_id(0) == 0)
    def _transpose_value_weights():
        wvt_ref[...] = w_ref[:, v0:u0].astype(F32).T.astype(BF16)
        wvct_ref[...] = w_ref[:, vc0:].astype(F32).T.astype(BF16)

    h = _rms(x_ref[...], g_ref[...]).astype(BF16)
    qk = jnp.dot(h, w_ref[:, :v0], preferred_element_type=F32)
    col = lax.broadcasted_iota(jnp.int32, (1, qk.shape[1]), 1)
    qscale = HEAD_DIM ** -0.5 * LOG2E
    qk_ref[...] = (qk * jnp.where(col < A_WIDTH, qscale, 1.0)).astype(BF16)
    vt = _dot_nt(wvt_ref[...], h).astype(BF16)
    rows = A_V + SUM_ROWS
    for t in range(vt_ref.shape[0]):
        for hd in range(A_HEADS):
            vt_ref[t, hd * rows:hd * rows + A_V, :] = vt[hd * A_V:(hd + 1) * A_V, t * tk:(t + 1) * tk]
            vt_ref[t, hd * rows + A_V:(hd + 1) * rows, :] = jnp.ones((SUM_ROWS, tk), BF16)
    u_ref[...] = jnp.dot(h, w_ref[:, u0:c0], preferred_element_type=F32)
    c = jnp.dot(h, w_ref[:, c0:vc0], preferred_element_type=F32)
    colc = lax.broadcasted_iota(jnp.int32, (1, c.shape[1]), 1)
    c_ref[...] = (c * jnp.where(colc < C_WIDTH, qscale, 1.0)).astype(BF16)
    vct = _dot_nt(wvct_ref[...], h).astype(BF16)
    rows = HEAD_DIM + SUM_ROWS
    for t in range(vct_ref.shape[0]):
        for hd in range(C_HEADS):
            vct_ref[t, hd * rows:hd * rows + HEAD_DIM, :] = (
                vct[hd * HEAD_DIM:(hd + 1) * HEAD_DIM, t * tc:(t + 1) * tc])
            vct_ref[t, hd * rows + HEAD_DIM:(hd + 1) * rows, :] = jnp.ones((SUM_ROWS, tc), BF16)


def _in_proj(x2, attn_norm, w_in_b, layer):
    n, d = x2.shape
    tm, tk, tc = TM_PROJ, TK_A, TQ_C
    lsel = lambda *_: (layer, 0, 0)
    arows = A_HEADS * (A_V + SUM_ROWS)
    crows = C_HEADS * (HEAD_DIM + SUM_ROWS)
    return pl.pallas_call(
        functools.partial(_in_proj_kernel, tk=tk, tc=tc),
        grid=(n // tm,),
        in_specs=[
            pl.BlockSpec((tm, d), lambda i: (i, 0)),
            pl.BlockSpec((None, 1, d), lsel),
            pl.BlockSpec((None,) + w_in_b.shape[1:], lsel),
        ],
        out_specs=[
            pl.BlockSpec((tm, 2 * A_WIDTH), lambda i: (i, 0)),
            pl.BlockSpec((tm // tk, arows, tk), lambda i: (i, 0, 0)),
            pl.BlockSpec((tm, B_WIDTH), lambda i: (i, 0)),
            pl.BlockSpec((tm, 2 * C_WIDTH), lambda i: (i, 0)),
            pl.BlockSpec((tm // tc, crows, tc), lambda i: (i, 0, 0)),
        ],
        out_shape=[
            jax.ShapeDtypeStruct((n, 2 * A_WIDTH), BF16),
            jax.ShapeDtypeStruct((n // tk, arows, tk), BF16),
            jax.ShapeDtypeStruct((n, B_WIDTH), F32),
            jax.ShapeDtypeStruct((n, 2 * C_WIDTH), BF16),
            jax.ShapeDtypeStruct((n // tc, crows, tc), BF16),
        ],
        scratch_shapes=[
            pltpu.VMEM((A_WIDTH, d), BF16),
            pltpu.VMEM((C_WIDTH, d), BF16),
        ],
        compiler_params=pltpu.CompilerParams(
            dimension_semantics=("arbitrary",), vmem_limit_bytes=VMEM_LIMIT,
            allow_input_fusion=(False, False, True)),
        name="in_proj",
    )(x2, attn_norm, w_in_b)


def _attn_a_kernel(lam_ref, gain_ref, q_ref, k_ref, vt_ref, o_ref,
                   qcat_ref, sa_ref, sb_ref, m_ref, l_ref, acc_ref,
                   *, tq, tk, strip, nq, lam_init):
    nstrip = 2 * tq // strip

    def load_queries(i):
        q = q_ref[pl.ds(pl.multiple_of(i * tq, tq), tq), :]
        lane = lax.broadcasted_iota(jnp.int32, q.shape, 1)
        zero = jnp.zeros_like(q)
        qcat_ref[0:tq, :] = jnp.where(lane < HEAD_DIM, q, zero)
        qcat_ref[tq:2 * tq, :] = jnp.where(lane >= HEAD_DIM, q, zero)

    def reset_state():
        m_ref[...] = jnp.full(m_ref.shape, NEG, F32)
        l_ref[...] = jnp.zeros(l_ref.shape, F32)
        acc_ref[...] = jnp.zeros(acc_ref.shape, F32)

    def visible(c, diag):
        qblock = ((c * strip) % tq) // tk
        return 2 if diag is None or qblock > diag else int(qblock == diag)

    def pieces(c, diag):
        kind = visible(c, diag)
        if kind == 0:
            return []
        if kind == 2:
            return [(c * strip, strip, tk, False)]
        out = []
        for h in range(strip // DIAG_SUB):
            off = (c * strip) % tk + h * DIAG_SUB
            out.append((c * strip + h * DIAG_SUB, DIAG_SUB, off + DIAG_SUB, True))
        return out

    def scores(j, c, dst_ref, diag=None):
        for col, width, keys, _ in pieces(c, diag):
            k_t = k_ref[pl.ds(pl.multiple_of(j * tk, tk), keys), :]
            dst_ref[0:keys, col:col + width] = _dot_nt(k_t, qcat_ref[col:col + width, :])

    def consume(t, c, src_ref, diag=None):
        for col, width, keys, masked in pieces(c, diag):
            cs = slice(col, col + width)
            s = src_ref[0:keys, cs]
            if masked:
                kc = lax.broadcasted_iota(jnp.int32, s.shape, 0) // CHUNK
                qc = (col % tk + lax.broadcasted_iota(jnp.int32, s.shape, 1)) // CHUNK
                s = jnp.where(kc <= qc, s, NEG)
            m_old = m_ref[:, cs]
            m_new = jnp.maximum(m_old, jnp.max(s, axis=0, keepdims=True))
            alpha = jnp.exp2(m_old - m_new)
            p = jnp.exp2(s - m_new).astype(BF16)
            pv = jnp.dot(vt_ref[t, :, 0:keys], p, preferred_element_type=F32)
            l_ref[:, cs] = alpha * l_ref[:, cs] + pv[A_V:A_V + 1, :]
            acc_ref[:, cs] = alpha * acc_ref[:, cs] + pv[:A_V, :]
            m_ref[:, cs] = m_new

    def pipelined(j, dst_ref, src_ref, diag_next=None, diag=None):
        for c in range(nstrip):
            scores(j, c, dst_ref, diag_next)
            consume(j - 1, c, src_ref, diag)

    def finalize(i):
        lv = lam_ref[...]
        lam = (jnp.exp(jnp.sum(lv[0:1] * lv[1:2], axis=1, keepdims=True))
               - jnp.exp(jnp.sum(lv[2:3] * lv[3:4], axis=1, keepdims=True)) + lam_init)
        inv = 1.0 / l_ref[...]
        a = (acc_ref[:, :tq] * inv[:, :tq]
             - acc_ref[:, tq:] * (lam * inv[:, tq:]))
        ms = jnp.mean(a * a, axis=0, keepdims=True)
        y = a * lax.rsqrt(ms + EPS) * (gain_ref[...] * (1.0 - lam_init))
        o_ref[pl.ds(pl.multiple_of(i * tq, tq), tq), :] = y.T.astype(o_ref.dtype)

    def pair(mm, carry):
        pipelined(2 * mm - 1, sb_ref, sa_ref)
        pipelined(2 * mm, sa_ref, sb_ref)
        return carry

    ratio = tq // tk
    assert ratio % 4 == 0
    load_queries(0)
    reset_state()
    for c in range(nstrip):
        scores(0, c, sa_ref)

    def query_tile(i, carry):
        first = ratio * i
        lax.fori_loop(0, first // 4, lambda n, cr: pair(2 * n + 2, pair(2 * n + 1, cr)), 0)
        bufs = (sa_ref, sb_ref)
        for r in range(ratio - 1):
            pipelined(first + r + 1, bufs[(r + 1) % 2], bufs[r % 2], diag_next=r + 1, diag=r)
        load_queries(jnp.minimum(i + 1, nq - 1))
        for c in range(nstrip):
            scores(0, c, sa_ref)
            consume(first + ratio - 1, c, sb_ref, diag=ratio - 1)
        finalize(i)
        reset_state()
        return carry

    lax.fori_loop(0, nq, query_tile, 0)


def _attn_a(qk, vt, lamvec, gain, layer, bsz, s_len):
    n = qk.shape[0]
    tq, tk = TQ_A, TK_A
    assert tq % (2 * tk) == 0 and tk % STRIP_A == 0
    nq = s_len // tq
    lam_init = 0.8 - 0.6 * math.exp(-0.3 * layer)
    return pl.pallas_call(
        functools.partial(_attn_a_kernel, tq=tq, tk=tk, strip=STRIP_A, nq=nq, lam_init=lam_init),
        grid=(bsz, A_HEADS),
        in_specs=[
            pl.BlockSpec((None, 4, HEAD_DIM), lambda b, h: (layer, 0, 0)),
            pl.BlockSpec((None, A_V, 1), lambda b, h: (layer, 0, 0)),
            pl.BlockSpec((s_len, A_V), lambda b, h: (b, h)),
            pl.BlockSpec((s_len, A_V), lambda b, h: (b, A_HEADS + h)),
            pl.BlockSpec((s_len // tk, A_V + SUM_ROWS, tk), lambda b, h: (b, h, 0)),
        ],
        out_specs=pl.BlockSpec((s_len, A_V), lambda b, h: (b, h)),
        out_shape=jax.ShapeDtypeStruct((n, A_WIDTH), BF16),
        scratch_shapes=[
            pltpu.VMEM((2 * tq, A_V), BF16),
            pltpu.VMEM((tk, 2 * tq), F32),
            pltpu.VMEM((tk, 2 * tq), F32),
            pltpu.VMEM((1, 2 * tq), F32),
            pltpu.VMEM((1, 2 * tq), F32),
            pltpu.VMEM((A_V, 2 * tq), F32),
        ],
        compiler_params=pltpu.CompilerParams(
            dimension_semantics=("arbitrary", "arbitrary"),
            vmem_limit_bytes=VMEM_LIMIT),
        name="attn_a",
    )(lamvec, gain, qk, qk, vt)


def _mix_bc_kernel(rb_ref, pw_ref, ps_ref, u_ref, up_ref, q_ref, kp_ref, kc_ref, vp_ref, vc_ref,
                   o_ref, bias_ref, s_ref, *, tq, sub):
    j = pl.program_id(1)
    nk = 3 * tq
    rows_u = sub * tq

    @pl.when((pl.program_id(0) == 0) & (j == 0))
    def _build_bias():
        rb = rb_ref[...]
        lane = lax.broadcasted_iota(jnp.int32, rb.shape, 1)
        edge = rb[:, 2 * REL_CLIP:2 * REL_CLIP + 1]
        ext = jnp.where(lane > 2 * REL_CLIP, edge, rb)
        kc = lax.broadcasted_iota(jnp.int32, (nk, tq), 0) // CHUNK
        qc = lax.broadcasted_iota(jnp.int32, (nk, tq), 1) // CHUNK
        band = (kc >= qc) & (kc <= qc + C_LEFT_CHUNKS)
        for h in range(C_HEADS):
            rows = jnp.broadcast_to(ext[h:h + 1, :], (nk, ext.shape[1]))
            rolled = pltpu.roll(rows, REL_CLIP, axis=1, stride=1, stride_axis=0)
            bias_ref[h] = jnp.where(band, rolled[:, :tq] * LOG2E, NEG)

    u = u_ref[...]
    halo = jnp.where(j > 0, up_ref[...], 0.0)
    ext_u = jnp.concatenate([halo, u], axis=0)
    sums, w = {1: ext_u}, 1
    while w < max(POOL_WINDOWS):
        sums[2 * w] = sums[w] + pltpu.roll(sums[w], w, axis=0)
        w *= 2
    grp = lax.broadcasted_iota(jnp.int32, (1, B_WIDTH), 1) // B_GROUP_DIM
    wsum, win = sums[POOL_WINDOWS[-1]], POOL_WINDOWS[-1]
    for g in reversed(range(B_GROUPS - 1)):
        wsum = jnp.where(grp == g, sums[POOL_WINDOWS[g]], wsum)
        win = jnp.where(grp == g, POOL_WINDOWS[g], win)
    wsum = wsum[POOL_HALO:, :]
    pos = j * rows_u + lax.broadcasted_iota(jnp.int32, (rows_u, 1), 0)
    cnt = jnp.minimum(pos + 1, win).astype(F32)
    d = wsum / cnt - u
    yb = jnp.dot(d.astype(BF16), pw_ref[...], preferred_element_type=F32) * ps_ref[...]
    o_ref[:, 0:B_WIDTH] = yb.astype(o_ref.dtype)

    assert sub == 2
    key_tiles = [kp_ref[0:tq, :], kp_ref[tq:2 * tq, :], kc_ref[0:tq, :], kc_ref[tq:2 * tq, :]]
    val_tiles = [(vp_ref, 0), (vp_ref, 1), (vc_ref, 0), (vc_ref, 1)]
    head = lax.broadcasted_iota(jnp.int32, (1, C_WIDTH), 1) // HEAD_DIM
    for t in range(sub):
        q = q_ref[t * tq:(t + 1) * tq, :]
        kk = jnp.concatenate(key_tiles[t:t + 3], axis=0)
        for h in range(C_HEADS):
            qh = jnp.where(head == h, q, jnp.zeros_like(q))
            s_ref[t * C_HEADS + h] = _dot_nt(kk, qh)
    rows = HEAD_DIM + SUM_ROWS
    krow = lax.broadcasted_iota(jnp.int32, (nk, tq), 0)
    for t in range(sub):
        kneg = jnp.where(krow >= (2 - (sub * j + t)) * tq, 0.0, NEG)
        outs = []
        for h in range(C_HEADS):
            s = s_ref[t * C_HEADS + h] + bias_ref[h] + kneg
            m = jnp.max(s, axis=0, keepdims=True)
            p = jnp.exp2(s - m).astype(BF16)
            pv = None
            for w in range(3):
                vref, vi = val_tiles[t + w]
                part = jnp.dot(vref[vi, h * rows:(h + 1) * rows, :], p[w * tq:(w + 1) * tq, :],
                               preferred_element_type=F32)
                pv = part if pv is None else pv + part
            outs.append(pv[:HEAD_DIM, :] / pv[HEAD_DIM:HEAD_DIM + 1, :])
        out = jnp.concatenate(outs, axis=0).T
        o_ref[t * tq:(t + 1) * tq, B_WIDTH:B_WIDTH + C_WIDTH] = out.astype(o_ref.dtype)


def _mix_bc(u, c, vct, rbp, pw, ps, layer, bsz, s_len):
    n = u.shape[0]
    tq, sub = TQ_C, 2
    assert 2 * tq == C_LEFT_CHUNKS * CHUNK and rbp.shape[-1] == 4 * tq
    rows = sub * tq
    nt = s_len // rows
    hpt = rows // POOL_HALO
    crows = vct.shape[1]
    cur = lambda col: (lambda b, j: (b * nt + j, col))
    prev = lambda col: (lambda b, j: (b * nt + jnp.maximum(j - 1, 0), col))
    return pl.pallas_call(
        functools.partial(_mix_bc_kernel, tq=tq, sub=sub),
        grid=(bsz, nt),
        in_specs=[
            pl.BlockSpec((None,) + rbp.shape[1:], lambda b, j: (layer, 0, 0)),
            pl.BlockSpec((None,) + pw.shape[1:], lambda b, j: (layer, 0, 0)),
            pl.BlockSpec((None, 1, B_WIDTH), lambda b, j: (layer, 0, 0)),
            pl.BlockSpec((rows, B_WIDTH), cur(0)),
            pl.BlockSpec((POOL_HALO, B_WIDTH),
                         lambda b, j: ((b * nt + j) * hpt - jnp.minimum(j, 1), 0)),
            pl.BlockSpec((rows, C_WIDTH), cur(0)),
            pl.BlockSpec((rows, C_WIDTH), prev(1)),
            pl.BlockSpec((rows, C_WIDTH), cur(1)),
            pl.BlockSpec((sub, crows, tq), lambda b, j: (b * nt + jnp.maximum(j - 1, 0), 0, 0)),
            pl.BlockSpec((sub, crows, tq), lambda b, j: (b * nt + j, 0, 0)),
        ],
        out_specs=pl.BlockSpec((rows, B_WIDTH + C_WIDTH), lambda b, j: (b * nt + j, 0)),
        out_shape=jax.ShapeDtypeStruct((n, B_WIDTH + C_WIDTH), BF16),
        scratch_shapes=[pltpu.VMEM((C_HEADS, 3 * tq, tq), F32),
                        pltpu.VMEM((sub * C_HEADS, 3 * tq, tq), F32)],
        compiler_params=pltpu.CompilerParams(
            dimension_semantics=("arbitrary", "arbitrary"), vmem_limit_bytes=VMEM_LIMIT),
        name="mix_bc",
    )(rbp, pw, ps, u, u, c, c, c, vct, vct)


def _post_kernel(x_ref, a_ref, bc_ref, wo_ref, g_ref, wup_ref, cw_ref, cb_ref, wd_ref, gf_ref,
                 o_ref, h_ref, acc_ref, carry_ref, a0_ref, g0_ref, a1_ref, g1_ref, hid_ref,
                 *, tiles_per_seq, final_norm):
    i = pl.program_id(0)
    tm = x_ref.shape[0]
    d_ff = wd_ref.shape[0]
    fc = a0_ref.shape[1]
    nch = d_ff // fc

    def cols(c, base=0):
        start = base + c * fc
        return pl.ds(start if isinstance(start, int) else pl.multiple_of(start, fc), fc)

    xm = (x_ref[...]
          + jnp.dot(a_ref[...], wo_ref[0:A_WIDTH, :], preferred_element_type=F32)
          + jnp.dot(bc_ref[...], wo_ref[A_WIDTH:, :], preferred_element_type=F32))
    h_ref[...] = _rms(xm, g_ref[...]).astype(BF16)
    acc_ref[...] = xm
    seq_start = (i % tiles_per_seq) == 0
    row = lax.broadcasted_iota(jnp.int32, (8, 1), 0)

    def up(c, a_ref, g_ref):
        h = h_ref[...]
        a_ref[...] = jnp.dot(h, wup_ref[:, cols(c)], preferred_element_type=F32)
        g_ref[...] = jnp.dot(h, wup_ref[:, cols(c, d_ff)], preferred_element_type=F32)

    def params(c):
        tail = jnp.where(seq_start, 0.0, carry_ref[c])
        return tail, cw_ref[:, cols(c)], cb_ref[:, cols(c)]

    def gate(c, a_ref, g_ref, prm):
        tail, cw, cb = prm
        a = a_ref[...]
        carry_ref[c] = a[tm - 8:, :]
        r1 = pltpu.roll(a, 1, axis=0)
        r2 = pltpu.roll(a, 2, axis=0)
        top1 = jnp.where(row == 0, tail[7:8, :], r1[:8, :])
        top2 = jnp.where(row == 0, tail[6:7, :], jnp.where(row == 1, tail[7:8, :], r2[:8, :]))
        a1 = jnp.concatenate([top1, r1[8:, :]], axis=0)
        a2 = jnp.concatenate([top2, r2[8:, :]], axis=0)
        conv = cb + cw[0:1, :] * a2
        conv = conv + cw[1:2, :] * a1
        conv = conv + cw[2:3, :] * a
        hid_ref[:, cols(c)] = (conv * jax.nn.sigmoid(conv) * g_ref[...]).astype(BF16)

    assert nch % 2 == 1 and nch >= 3
    up(0, a0_ref, g0_ref)
    up(1, a1_ref, g1_ref)

    def pair(pp, carry):
        c = 2 * pp
        prm0, prm1 = params(c), params(c + 1)
        gate(c, a0_ref, g0_ref, prm0)
        up(c + 2, a0_ref, g0_ref)
        gate(c + 1, a1_ref, g1_ref, prm1)
        up(c + 3, a1_ref, g1_ref)
        return carry

    npair = (nch - 3) // 2
    assert npair % 2 == 0
    for pp in range(npair):
        pair(pp, 0)
    gate(nch - 3, a0_ref, g0_ref, params(nch - 3))
    up(nch - 1, a0_ref, g0_ref)
    gate(nch - 2, a1_ref, g1_ref, params(nch - 2))
    gate(nch - 1, a0_ref, g0_ref, params(nch - 1))
    out = acc_ref[...] + jnp.dot(hid_ref[...], wd_ref[...], preferred_element_type=F32)
    if final_norm:
        out = _rms(out, gf_ref[...])
    o_ref[...] = out


def _post(x2, oa, obc, wo, ffn_norm, wup, cw, cb, wd, gf, layer, s_len, final_norm):
    n, d = x2.shape
    tm, fc = TM_POST, FF_CHUNK
    d_ff = wd.shape[1]
    nch = d_ff // fc
    const = pl.Buffered(1)
    l3 = lambda i: (layer, 0, 0)
    return pl.pallas_call(
        functools.partial(_post_kernel, tiles_per_seq=s_len // tm, final_norm=final_norm),
        grid=(n // tm,),
        in_specs=[
            pl.BlockSpec((tm, d), lambda i: (i, 0)),
            pl.BlockSpec((tm, A_WIDTH), lambda i: (i, 0)),
            pl.BlockSpec((tm, B_WIDTH + C_WIDTH), lambda i: (i, 0)),
            pl.BlockSpec((None,) + wo.shape[1:], l3, pipeline_mode=const),
            pl.BlockSpec((None, 1, d), l3),
            pl.BlockSpec((None,) + wup.shape[1:], l3, pipeline_mode=const),
            pl.BlockSpec((None,) + cw.shape[1:], l3),
            pl.BlockSpec((None,) + cb.shape[1:], l3),
            pl.BlockSpec((None,) + wd.shape[1:], l3, pipeline_mode=const),
            pl.BlockSpec((1, d), lambda i: (0, 0)),
        ],
        out_specs=pl.BlockSpec((tm, d), lambda i: (i, 0)),
        out_shape=jax.ShapeDtypeStruct((n, d), F32),
        scratch_shapes=[
            pltpu.VMEM((tm, d), BF16),
            pltpu.VMEM((tm, d), F32),
            pltpu.VMEM((nch, 8, fc), F32),
            pltpu.VMEM((tm, fc), F32), pltpu.VMEM((tm, fc), F32),
            pltpu.VMEM((tm, fc), F32), pltpu.VMEM((tm, fc), F32),
            pltpu.VMEM((tm, d_ff), BF16),
        ],
        compiler_params=pltpu.CompilerParams(
            dimension_semantics=("arbitrary",), vmem_limit_bytes=VMEM_LIMIT),
        name="post",
    )(x2, oa, obc, wo, ffn_norm, wup, cw, cb, wd, gf)


def kernel(x, attn_norm, w_in, lam_q1, lam_k1, lam_q2, lam_k2, diff_subln, pool_w, pool_scale,
           rel_bias, w_out, ffn_norm, w_up, conv_w, conv_b, w_down, final_norm):
    bsz, s_len, d = x.shape
    depth = w_in.shape[0]
    d_ff = w_down.shape[1]
    assert d_ff % FF_CHUNK == 0 and s_len % TQ_A == 0 and s_len % TM_POST == 0
    assert all(w & (w - 1) == 0 and w <= POOL_HALO for w in POOL_WINDOWS) and len(POOL_WINDOWS) == B_GROUPS

    w_in_b = w_in.astype(BF16)
    wo = w_out.astype(BF16)
    wup = w_up.astype(BF16)
    wd = w_down.astype(BF16)
    cw = conv_w
    cb = conv_b[:, None, :]
    lamvec = jnp.stack([lam_q1, lam_k1, lam_q2, lam_k2], axis=1)
    gain = diff_subln[:, :, None]
    eye = jnp.eye(B_GROUPS, dtype=pool_w.dtype)
    pw = jnp.einsum('lgcd,gh->lgchd', pool_w, eye).reshape(depth, B_WIDTH, B_WIDTH).astype(BF16)
    ps = pool_scale[:, None, :]
    rbp = jnp.pad(rel_bias, ((0, 0), (0, 0), (0, 4 * TQ_C - rel_bias.shape[-1])))
    an = attn_norm[:, None, :]
    fn = ffn_norm[:, None, :]
    gf = final_norm[None, :]

    x2 = x.reshape(bsz * s_len, d)
    for layer in range(depth):
        qk, vt, u, c, vct = _in_proj(x2, an, w_in_b, layer)
        oa = _attn_a(qk, vt, lamvec, gain, layer, bsz, s_len)
        obc = _mix_bc(u, c, vct, rbp, pw, ps, layer, bsz, s_len)
        x2 = _post(x2, oa, obc, wo, fn, wup, cw, cb, wd, gf, layer, s_len,
                   final_norm=(layer == depth - 1))
    return x2.reshape(bsz, s_len, d)
```
